```python
import math
import jax, jax.numpy as jnp
from jax import lax
import numpy as np

D_MODEL = 1024
BATCH = 8
SEQ = 8192
DEPTH = 4

HEAD_DIM = 64
N_Q_HEADS = 16
N_KV_HEADS = 4
GROUP = N_Q_HEADS // N_KV_HEADS
WINDOW = 128
BLOCK = 128
ROT_DIM = HEAD_DIM // 4
ROPE_THETA = 500000.0
Q_WIDTH = N_Q_HEADS * HEAD_DIM
KV_WIDTH = N_KV_HEADS * HEAD_DIM
D_RNN = 1024
RNN_BLOCKS = 16
RNN_BW = D_RNN // RNN_BLOCKS
CONV_WIDTH = 4
LRU_C = 8.0
POOL_WINDOWS = (2, 4, 8, 16)
N_POOL_GROUPS = len(POOL_WINDOWS)
D_POOL = 1024
POOL_GW = D_POOL // N_POOL_GROUPS
N_BRANCHES = 3
D_FF = 4 * D_MODEL
EPS = 1e-6
NEG_INF = -1e30

D_IN = Q_WIDTH + 2 * KV_WIDTH + 2 * D_RNN + D_POOL + N_BRANCHES * D_MODEL
SPLITS = tuple(np.cumsum([Q_WIDTH, KV_WIDTH, KV_WIDTH, D_RNN, D_RNN, D_POOL]).tolist())

kernel_name = "hybrid_swa_rglru_pool_gated_block"


def rms_norm(x, g):
    xf = x.astype(jnp.float32)
    y = xf * lax.rsqrt(jnp.mean(xf * xf, axis=-1, keepdims=True) + EPS)
    return (y * g.astype(jnp.float32)).astype(x.dtype)


def rope_tables(seq):
    inv_freq = ROPE_THETA ** (-jnp.arange(0, ROT_DIM, 2, dtype=jnp.float32) / ROT_DIM)
    ang = jnp.arange(seq, dtype=jnp.float32)[:, None] * inv_freq[None, :]
    return jnp.cos(ang)[:, None, :], jnp.sin(ang)[:, None, :]


def apply_partial_rope(x, cos, sin):
    half = ROT_DIM // 2
    xf = x.astype(jnp.float32)
    x1, x2, rest = xf[..., :half], xf[..., half:ROT_DIM], xf[..., ROT_DIM:]
    out = jnp.concatenate([x1 * cos - x2 * sin, x2 * cos + x1 * sin, rest], axis=-1)
    return out.astype(x.dtype)


def sliding_window_attention(q, k, v, sinks):
    b, s = q.shape[0], q.shape[1]
    nb = s // BLOCK
    qb = q.reshape(b, nb, BLOCK, N_KV_HEADS, GROUP, HEAD_DIM)
    kb = k.reshape(b, nb, BLOCK, N_KV_HEADS, HEAD_DIM)
    vb = v.reshape(b, nb, BLOCK, N_KV_HEADS, HEAD_DIM)
    pad = ((0, 0), (1, 0), (0, 0), (0, 0), (0, 0))
    kcat = jnp.concatenate([jnp.pad(kb[:, :-1], pad), kb], axis=2)
    vcat = jnp.concatenate([jnp.pad(vb[:, :-1], pad), vb], axis=2)
    scores = jnp.einsum('bnqhgd,bnkhd->bnhgqk', qb, kcat).astype(jnp.float32) * (HEAD_DIM ** -0.5)
    qi = jnp.arange(BLOCK)[:, None]
    ki = jnp.arange(2 * BLOCK)[None, :]
    rel = qi + BLOCK - ki
    band = (rel >= 0) & (rel < WINDOW)
    has_prev = (jnp.arange(nb) > 0)[:, None, None]
    valid = band[None] & (has_prev | (ki >= BLOCK)[None])
    scores = jnp.where(valid[None, :, None, None], scores, NEG_INF)
    sink = sinks.astype(jnp.float32).reshape(N_KV_HEADS, GROUP)[None, None, :, :, None, None]
    m = jnp.maximum(jnp.max(scores, axis=-1, keepdims=True), sink)
    p = jnp.exp(scores - m)
    p = p / (jnp.sum(p, axis=-1, keepdims=True) + jnp.exp(sink - m))
    o = jnp.einsum('bnhgqk,bnkhd->bnqhgd', p.astype(v.dtype), vcat)
    return o.reshape(b, s, Q_WIDTH)


def causal_depthwise_conv(x, w, bias):
    s = x.shape[1]
    xp = jnp.pad(x, ((0, 0), (CONV_WIDTH - 1, 0), (0, 0)))
    y = bias
    for tap in range(CONV_WIDTH):
        y = y + xp[:, tap:tap + s] * w[tap]
    return y


def _lru_combine(e1, e2):
    a1, b1 = e1
    a2, b2 = e2
    return a1 * a2, a2 * b1 + b2


def rg_lru(x, w_a, b_a, w_i, b_i, lam):
    b, s, c = x.shape
    xf = x.astype(jnp.float32)
    xb = xf.reshape(b, s, RNN_BLOCKS, RNN_BW)
    r = jax.nn.sigmoid(jnp.einsum('bshi,hij->bshj', xb, w_a.astype(jnp.float32)).reshape(b, s, c) + b_a)
    i = jax.nn.sigmoid(jnp.einsum('bshi,hij->bshj', xb, w_i.astype(jnp.float32)).reshape(b, s, c) + b_i)
    log_a = LRU_C * r * jax.nn.log_sigmoid(lam.astype(jnp.float32))
    a = jnp.exp(log_a)
    u = jnp.sqrt(-jnp.expm1(2.0 * log_a)) * (i * xf)
    _, h = lax.associative_scan(_lru_combine, (a, u), axis=1)
    return h


def multi_scale_pool(p, w_groups, scale):
    b, s, c = p.shape
    pf = p.astype(jnp.float32)
    cs = jnp.cumsum(pf, axis=1)
    t = jnp.arange(s)
    outs = []
    for g, w in enumerate(POOL_WINDOWS):
        sl = slice(g * POOL_GW, (g + 1) * POOL_GW)
        c_g = cs[..., sl]
        lag = jnp.pad(c_g[:, :s - w], ((0, 0), (w, 0), (0, 0)))
        cnt = jnp.minimum(t + 1, w).astype(jnp.float32)[None, :, None]
        outs.append((c_g - lag) / cnt - pf[..., sl])
    pooled = jnp.stack(outs, axis=2)
    mixed = jnp.einsum('bsgi,gij->bsgj', pooled, w_groups.astype(jnp.float32)).reshape(b, s, c)
    return (mixed * scale.astype(jnp.float32)).astype(p.dtype)


def setup_inputs(seed: int = 0) -> dict:
    key = jax.random.key(seed)
    ks = jax.random.split(key, 24)
    f32 = jnp.float32

    def nrm(k, shape, fan_in):
        return jax.random.normal(k, shape, f32) * (fan_in ** -0.5)

    def gain(k, shape):
        return 1.0 + 0.05 * jax.random.normal(k, shape, f32)

    L = DEPTH
    a0 = jax.random.uniform(ks[11], (L, D_RNN), f32, 0.9, 0.999)
    return {
        "x": jax.random.normal(ks[0], (BATCH, SEQ, D_MODEL), f32),
        "norm_mix_pre": gain(ks[1], (L, D_MODEL)),
        "norm_mix_post": gain(ks[2], (L, D_MODEL)),
        "w_in": nrm(ks[3], (L, D_MODEL, D_IN), D_MODEL),
        "attn_sinks": 0.5 * jax.random.normal(ks[4], (L, N_Q_HEADS), f32),
        "w_attn_br": nrm(ks[5], (L, Q_WIDTH, D_MODEL), Q_WIDTH),
        "conv_w": nrm(ks[6], (L, CONV_WIDTH, D_RNN), CONV_WIDTH),
        "conv_b": 0.01 * jax.random.normal(ks[7], (L, D_RNN), f32),
        "w_rg_a": nrm(ks[8], (L, RNN_BLOCKS, RNN_BW, RNN_BW), RNN_BW),
        "b_rg_a": 0.01 * jax.random.normal(ks[9], (L, D_RNN), f32),
        "w_rg_i": nrm(ks[10], (L, RNN_BLOCKS, RNN_BW, RNN_BW), RNN_BW),
        "b_rg_i": 0.01 * jax.random.normal(ks[12], (L, D_RNN), f32),
        "lru_lambda": jnp.log(a0) - jnp.log1p(-a0),
        "w_rnn_br": nrm(ks[13], (L, D_RNN, D_MODEL), D_RNN),
        "w_pool_groups": nrm(ks[14], (L, N_POOL_GROUPS, POOL_GW, POOL_GW), POOL_GW),
        "pool_scale": gain(ks[15], (L, D_POOL)),
        "w_pool_br": nrm(ks[16], (L, D_POOL, D_MODEL), D_POOL),
        "w_out": nrm(ks[17], (L, D_MODEL, D_MODEL), D_MODEL),
        "norm_mlp_pre": gain(ks[18], (L, D_MODEL)),
        "norm_mlp_post": gain(ks[19], (L, D_MODEL)),
        "w_mlp_up": nrm(ks[20], (L, D_MODEL, D_FF), D_MODEL),
        "w_mlp_down": nrm(ks[21], (L, D_FF, D_MODEL), D_FF),
    }


def reference(x, norm_mix_pre, norm_mix_post, w_in, attn_sinks, w_attn_br, conv_w, conv_b,
              w_rg_a, b_rg_a, w_rg_i, b_rg_i, lru_lambda, w_rnn_br, w_pool_groups, pool_scale,
              w_pool_br, w_out, norm_mlp_pre, norm_mlp_post, w_mlp_up, w_mlp_down):
    b, s, _ = x.shape
    cos, sin = rope_tables(s)
    h = x
    for l in range(DEPTH):
        u = rms_norm(h, norm_mix_pre[l])
        proj = u @ w_in[l]
        q, k, v, xr, yr, pp, gt = jnp.split(proj, SPLITS, axis=-1)
        q = apply_partial_rope(q.reshape(b, s, N_Q_HEADS, HEAD_DIM), cos, sin)
        k = apply_partial_rope(k.reshape(b, s, N_KV_HEADS, HEAD_DIM), cos, sin)
        v = v.reshape(b, s, N_KV_HEADS, HEAD_DIM)
        attn_br = sliding_window_attention(q, k, v, attn_sinks[l]) @ w_attn_br[l]
        xc = causal_depthwise_conv(xr, conv_w[l], conv_b[l])
        hr = rg_lru(xc, w_rg_a[l], b_rg_a[l], w_rg_i[l], b_rg_i[l], lru_lambda[l])
        rnn_br = (hr * jax.nn.gelu(yr.astype(jnp.float32))).astype(h.dtype) @ w_rnn_br[l]
        pool_br = multi_scale_pool(pp, w_pool_groups[l], pool_scale[l]) @ w_pool_br[l]
        g = jax.nn.sigmoid(gt.astype(jnp.float32)).reshape(b, s, N_BRANCHES, D_MODEL)
        merged = (g[:, :, 0] * attn_br + g[:, :, 1] * rnn_br + g[:, :, 2] * pool_br).astype(h.dtype)
        mix = merged @ w_out[l]
        h = h + rms_norm(mix, norm_mix_post[l])
        m = rms_norm(h, norm_mlp_pre[l])
        y = jnp.square(jax.nn.relu(m @ w_mlp_up[l])) @ w_mlp_down[l]
        h = h + rms_norm(y, norm_mlp_post[l])
    return h
```

```python
import functools

import jax
import jax.numpy as jnp
from jax import lax
from jax.experimental import pallas as pl
from jax.experimental.pallas import tpu as pltpu

F32 = jnp.float32
BF16 = jnp.bfloat16

HEAD_DIM = 64
N_Q_HEADS = 16
N_KV_HEADS = 4
ATT_BLOCK = 128
ROT_DIM = HEAD_DIM // 4
ROPE_THETA = 500000.0
RNN_BW = 64
CONV_WIDTH = 4
LRU_C = 8.0
POOL_WINDOWS = (2, 4, 8, 16)
EPS = 1e-6
NEG_INF = -1e30

LANES = 128
SUBLANES = 8
CONV_CARRY = SUBLANES
POOL_CARRY = 32
VMEM_LIMIT_BYTES = 58 * 1024 * 1024

MIX_TILE = 256
MLP_TILE = 512
FF_CHUNK = 1024


def _dot(a, b):
    return jnp.dot(a, b, preferred_element_type=F32)


def _dot_nt(a, b):
    return lax.dot_general(a, b, (((1,), (1,)), ((), ())), preferred_element_type=F32)


def _rms(x, g):
    ms = jnp.mean(x * x, axis=-1, keepdims=True)
    return (x * lax.rsqrt(ms + EPS)) * g


def _rope(x, c, s1, s2):
    half = ROT_DIM // 2
    return x * c + pltpu.roll(x, LANES - half, 1) * s1 + pltpu.roll(x, half, 1) * s2


def _head_variants(x, kc):
    lane = lax.broadcasted_iota(jnp.int32, x.shape, 1)
    lo = lane < HEAD_DIM
    xs = pltpu.roll(x, HEAD_DIM, 1)
    zero = jnp.zeros_like(x)
    return {
        (2 * kc, 0): jnp.where(lo, x, zero),
        (2 * kc, 1): jnp.where(lo, zero, xs),
        (2 * kc + 1, 0): jnp.where(lo, xs, zero),
        (2 * kc + 1, 1): jnp.where(lo, zero, x),
    }


def _mixer_kernel(h_ref, tab_ref, gpre_ref, gpost_ref, win_ref, sink_ref, wattn_ref,
                  convw_ref, convb_ref, wg_ref, ba_ref, bi_ref, lam_ref, wrnn_ref,
                  wpg_ref, psc_ref, wpool_ref, wout_ref, o_ref,
                  u_s, q_s, k_s, v_s, ao_s, xb_s, xc_s, ga_s, gi_s, y_s, hst_s, ro_s,
                  pb_s, pa_s, pc_s, po_s, acc_s):
    si = pl.program_id(1)
    T, D = h_ref.shape
    nblk = T // ATT_BLOCK

    @pl.when(si == 0)
    def _():
        k_s[:, 0:ATT_BLOCK, :] = jnp.zeros((2 * N_KV_HEADS, ATT_BLOCK, LANES), BF16)
        v_s[:, 0:ATT_BLOCK, :] = jnp.zeros((2 * N_KV_HEADS, ATT_BLOCK, LANES), BF16)
        xb_s[0:CONV_CARRY, :] = jnp.zeros((CONV_CARRY, D), F32)
        pb_s[0:POOL_CARRY, :] = jnp.zeros((POOL_CARRY, D), F32)
        hst_s[...] = jnp.zeros(hst_s.shape, F32)

    x = h_ref[...]
    u_s[...] = _rms(x, gpre_ref[...]).astype(BF16)

    c_q, s1_q, s2_q = tab_ref[0], tab_ref[1], tab_ref[2]
    inv_scale = float(HEAD_DIM) ** 0.5
    c_k, s1_k, s2_k = c_q * inv_scale, s1_q * inv_scale, s2_q * inv_scale

    q = _dot(u_s[...], win_ref[:, 0:D])
    for c in range(D // LANES):
        sl = slice(c * LANES, (c + 1) * LANES)
        q_s[:, sl] = _rope(q[:, sl], c_q, s1_q, s2_q).astype(BF16)

    kvw = N_KV_HEADS * HEAD_DIM
    k = _dot(u_s[...], win_ref[:, D:D + kvw])
    v = _dot(u_s[...], win_ref[:, D + kvw:D + 2 * kvw])
    cur = slice(ATT_BLOCK, ATT_BLOCK + T)
    for kc in range(kvw // LANES):
        sl = slice(kc * LANES, (kc + 1) * LANES)
        for (head, var), val in _head_variants(_rope(k[:, sl], c_k, s1_k, s2_k), kc).items():
            k_s[2 * head + var, cur, :] = val.astype(BF16)
        for (head, var), val in _head_variants(v[:, sl], kc).items():
            v_s[2 * head + var, cur, :] = val.astype(BF16)

    row = lax.broadcasted_iota(jnp.int32, (2 * ATT_BLOCK, ATT_BLOCK), 0)
    col = lax.broadcasted_iota(jnp.int32, (2 * ATT_BLOCK, ATT_BLOCK), 1)
    qpos = jnp.where(row >= ATT_BLOCK, row - ATT_BLOCK, row)
    from_prev = col > qpos
    top = row < ATT_BLOCK

    def attn_block(b, carry):
        r0 = pl.multiple_of(b * ATT_BLOCK, ATT_BLOCK)
        has_prev = (si * nblk + b) > 0
        for j in range(N_KV_HEADS):
            qst = jnp.concatenate(
                [q_s[pl.ds(r0, ATT_BLOCK), (2 * j) * LANES:(2 * j + 1) * LANES],
                 q_s[pl.ds(r0, ATT_BLOCK), (2 * j + 1) * LANES:(2 * j + 2) * LANES]], axis=0)
            o = None
            for var in range(2):
                kcat = k_s[2 * j + var, pl.ds(r0, 2 * ATT_BLOCK), :]
                s = _dot_nt(qst, kcat)
                s_prev = jnp.where(has_prev, s[:, 0:ATT_BLOCK], NEG_INF)
                sc = jnp.where(from_prev, s_prev, s[:, ATT_BLOCK:])
                sink = jnp.where(top, sink_ref[4 * j + var], sink_ref[4 * j + 2 + var])[:, 0:1]
                m = jnp.maximum(jnp.max(sc, axis=-1, keepdims=True), sink)
                p = jnp.exp(sc - m)
                denom = jnp.sum(p, axis=-1, keepdims=True) + jnp.exp(sink - m)
                p = p * (1.0 / denom)
                zero = jnp.zeros_like(p)
                pcat = jnp.concatenate(
                    [jnp.where(from_prev, p, zero), jnp.where(from_prev, zero, p)], axis=1)
                vcat = v_s[2 * j + var, pl.ds(r0, 2 * ATT_BLOCK), :]
                ov = _dot(pcat.astype(BF16), vcat)
                o = ov if o is None else o + ov
            ao_s[pl.ds(r0, ATT_BLOCK), (2 * j) * LANES:(2 * j + 1) * LANES] = (
                o[0:ATT_BLOCK].astype(BF16))
            ao_s[pl.ds(r0, ATT_BLOCK), (2 * j + 1) * LANES:(2 * j + 2) * LANES] = (
                o[ATT_BLOCK:].astype(BF16))
        return carry

    lax.fori_loop(0, nblk, attn_block, 0)
    k_s[:, 0:ATT_BLOCK, :] = k_s[:, T:T + ATT_BLOCK, :]
    v_s[:, 0:ATT_BLOCK, :] = v_s[:, T:T + ATT_BLOCK, :]

    base = D + 2 * kvw
    gate0 = base + 3 * D

    g = jax.nn.sigmoid(_dot(u_s[...], win_ref[:, gate0:gate0 + D]))
    acc_s[...] = g * _dot(ao_s[...], wattn_ref[...])

    xb_s[CONV_CARRY:CONV_CARRY + T, :] = _dot(u_s[...], win_ref[:, base:base + D])
    xc = convb_ref[...]
    for tap in range(CONV_WIDTH):
        off = CONV_CARRY - (CONV_WIDTH - 1) + tap
        xc = xc + xb_s[off:off + T, :] * convw_ref[tap:tap + 1, :]
    xc_s[...] = xc
    xb_s[0:CONV_CARRY, :] = xb_s[T:T + CONV_CARRY, :]

    for c in range(D // LANES):
        sl = slice(c * LANES, (c + 1) * LANES)
        res = _dot(xc_s[:, sl].astype(BF16), wg_ref[c])
        ga_s[:, sl] = res[:, 0:LANES]
        gi_s[:, sl] = res[:, LANES:]
    y_s[...] = _dot(u_s[...], win_ref[:, base + D:base + 2 * D])

    lam = lam_ref[...]
    ls8 = LRU_C * (jnp.minimum(lam, 0.0) - jnp.log1p(jnp.exp(-jnp.abs(lam))))
    ba = ba_ref[...]
    bi = bi_ref[...]
    rowi = lax.broadcasted_iota(jnp.int32, (SUBLANES, D), 0)

    def scan_rows(rows, hprev):
        r = jax.nn.sigmoid(ga_s[rows, :] + ba)
        i = jax.nn.sigmoid(gi_s[rows, :] + bi)
        log_a = r * ls8
        a = jnp.exp(log_a)
        uu = jnp.sqrt(jnp.tanh(-log_a) * (a * a + 1.0)) * (i * xc_s[rows, :])
        for kk in (1, 2, 4):
            keep = rowi >= kk
            a_sh = jnp.where(keep, pltpu.roll(a, kk, 0), 1.0)
            u_sh = jnp.where(keep, pltpu.roll(uu, kk, 0), 0.0)
            uu = a * u_sh + uu
            a = a * a_sh
        hrows = a * hprev + uu
        hlast = jnp.broadcast_to(hrows[SUBLANES - 1:SUBLANES, :], (SUBLANES, D))
        return hrows * jax.nn.gelu(y_s[rows, :]), hlast

    def scan_step(gidx, carry):
        r0 = pl.multiple_of(gidx * (2 * SUBLANES), 2 * SUBLANES)
        out0, h0 = scan_rows(pl.ds(r0, SUBLANES), hst_s[...])
        out1, h1 = scan_rows(pl.ds(r0 + SUBLANES, SUBLANES), h0)
        hst_s[...] = h1
        ro_s[pl.ds(r0, 2 * SUBLANES), :] = jnp.concatenate([out0, out1], axis=0).astype(BF16)
        return carry

    lax.fori_loop(0, T // (2 * SUBLANES), scan_step, 0)

    g = jax.nn.sigmoid(_dot(u_s[...], win_ref[:, gate0 + D:gate0 + 2 * D]))
    acc_s[...] = acc_s[...] + g * _dot(ro_s[...], wrnn_ref[...])

    P = POOL_CARRY
    pb_s[P:P + T, :] = _dot(u_s[...], win_ref[:, base + 2 * D:base + 3 * D])
    gw = D // len(POOL_WINDOWS)
    pa_s[8:P + T, :] = pb_s[8:P + T, :] + pb_s[7:P + T - 1, :]
    pc_s[16:P + T, gw:] = pa_s[16:P + T, gw:] + pa_s[14:P + T - 2, gw:]
    s2 = pa_s[P:P + T, 0:gw]
    s4 = pc_s[P:P + T, gw:2 * gw]
    pa_s[24:P + T, 2 * gw:] = pc_s[24:P + T, 2 * gw:] + pc_s[20:P + T - 4, 2 * gw:]
    s8 = pa_s[P:P + T, 2 * gw:3 * gw]
    s16 = pa_s[P:P + T, 3 * gw:] + pa_s[P - 8:P + T - 8, 3 * gw:]
    tpos = (si * T + lax.broadcasted_iota(jnp.int32, (T, 1), 0) + 1).astype(F32)
    for gi_, (w, ssum) in enumerate(zip(POOL_WINDOWS, (s2, s4, s8, s16))):
        sl = slice(gi_ * gw, (gi_ + 1) * gw)
        inv_cnt = 1.0 / jnp.minimum(tpos, float(w))
        pooled = ssum * inv_cnt - pb_s[P:P + T, sl]
        mixed = _dot(pooled.astype(BF16), wpg_ref[gi_])
        po_s[:, sl] = (mixed * psc_ref[:, sl]).astype(BF16)
    pb_s[0:P, :] = pb_s[T:T + P, :]

    g = jax.nn.sigmoid(_dot(u_s[...], win_ref[:, gate0 + 2 * D:gate0 + 3 * D]))
    merged = acc_s[...] + g * _dot(po_s[...], wpool_ref[...])

    mix = _dot(merged.astype(BF16), wout_ref[...])
    o_ref[...] = h_ref[...] + _rms(mix, gpost_ref[...])


def _mlp_kernel(h_ref, gpre_ref, gpost_ref, wup_ref, wdown_ref, o_ref, m_s, acc_s):
    d_ff = wup_ref.shape[1]
    m_s[...] = _rms(h_ref[...], gpre_ref[...]).astype(BF16)
    for c in range(d_ff // FF_CHUNK):
        sl = slice(c * FF_CHUNK, (c + 1) * FF_CHUNK)
        up = jnp.maximum(_dot(m_s[...], wup_ref[:, sl]), 0.0)
        part = _dot((up * up).astype(BF16), wdown_ref[sl, :])
        if c == 0:
            acc_s[...] = part
        else:
            acc_s[...] = acc_s[...] + part
    o_ref[...] = h_ref[...] + _rms(acc_s[...], gpost_ref[...])


def _const_spec(shape):
    nd = len(shape)
    return pl.BlockSpec(shape, lambda b, s: (0,) * nd, pipeline_mode=pl.Buffered(1))


def _mixer_call(h, tab, gpre, gpost, win, sinks, wattn, convw, convb, wg, ba, bi, lam,
                wrnn, wpg, psc, wpool, wout):
    B, S, D = h.shape
    T = MIX_TILE
    tile = pl.BlockSpec((None, T, D), lambda b, s: (b, s, 0))
    in_specs = [
        tile,
        pl.BlockSpec((3, T, LANES), lambda b, s: (0, s, 0)),
        _const_spec(gpre.shape), _const_spec(gpost.shape), _const_spec(win.shape),
        pl.BlockSpec(memory_space=pltpu.SMEM),
        _const_spec(wattn.shape), _const_spec(convw.shape), _const_spec(convb.shape),
        _const_spec(wg.shape), _const_spec(ba.shape), _const_spec(bi.shape),
        _const_spec(lam.shape), _const_spec(wrnn.shape), _const_spec(wpg.shape),
        _const_spec(psc.shape), _const_spec(wpool.shape), _const_spec(wout.shape),
    ]
    nvar = 2 * N_KV_HEADS
    scratch = [
        pltpu.VMEM((T, D), BF16),
        pltpu.VMEM((T, D), BF16),
        pltpu.VMEM((nvar, ATT_BLOCK + T, LANES), BF16),
        pltpu.VMEM((nvar, ATT_BLOCK + T, LANES), BF16),
        pltpu.VMEM((T, D), BF16),
        pltpu.VMEM((CONV_CARRY + T, D), F32),
        pltpu.VMEM((T, D), F32),
        pltpu.VMEM((T, D), F32),
        pltpu.VMEM((T, D), F32),
        pltpu.VMEM((T, D), F32),
        pltpu.VMEM((SUBLANES, D), F32),
        pltpu.VMEM((T, D), BF16),
        pltpu.VMEM((POOL_CARRY + T, D), F32),
        pltpu.VMEM((POOL_CARRY + T, D), F32),
        pltpu.VMEM((POOL_CARRY + T, D), F32),
        pltpu.VMEM((T, D), BF16),
        pltpu.VMEM((T, D), F32),
    ]
    return pl.pallas_call(
        _mixer_kernel,
        grid=(B, S // T),
        in_specs=in_specs,
        out_specs=tile,
        out_shape=jax.ShapeDtypeStruct(h.shape, h.dtype),
        scratch_shapes=scratch,
        compiler_params=pltpu.CompilerParams(
            dimension_semantics=("arbitrary", "arbitrary"),
            vmem_limit_bytes=VMEM_LIMIT_BYTES),
        name="mixer",
    )(h, tab, gpre, gpost, win, sinks, wattn, convw, convb, wg, ba, bi, lam, wrnn, wpg,
      psc, wpool, wout)


def _mlp_call(h, gpre, gpost, wup, wdown):
    B, S, D = h.shape
    T = MLP_TILE
    tile = pl.BlockSpec((None, T, D), lambda b, s: (b, s, 0))
    return pl.pallas_call(
        _mlp_kernel,
        grid=(B, S // T),
        in_specs=[tile, _const_spec(gpre.shape), _const_spec(gpost.shape),
                  _const_spec(wup.shape), _const_spec(wdown.shape)],
        out_specs=tile,
        out_shape=jax.ShapeDtypeStruct(h.shape, h.dtype),
        scratch_shapes=[pltpu.VMEM((T, D), BF16), pltpu.VMEM((T, D), F32)],
        compiler_params=pltpu.CompilerParams(
            dimension_semantics=("arbitrary", "arbitrary"),
            vmem_limit_bytes=VMEM_LIMIT_BYTES),
        name="mlp",
    )(h, gpre, gpost, wup, wdown)


def _rope_tables(seq):
    half = ROT_DIM // 2
    inv_freq = ROPE_THETA ** (-jnp.arange(0, ROT_DIM, 2, dtype=F32) / ROT_DIM)
    ang = jnp.arange(seq, dtype=F32)[:, None] * inv_freq[None, :]
    cos, sin = jnp.cos(ang), jnp.sin(ang)
    pad = jnp.zeros((seq, HEAD_DIM - ROT_DIM), F32)
    c = jnp.concatenate([cos, cos, pad + 1.0], axis=1)
    s1 = jnp.concatenate([-sin, jnp.zeros_like(sin), pad], axis=1)
    s2 = jnp.concatenate([jnp.zeros_like(sin), sin, pad], axis=1)
    tab = jnp.stack([c, s1, s2])
    tab = jnp.concatenate([tab] * (LANES // HEAD_DIM), axis=2)
    return tab * (float(HEAD_DIM) ** -0.5)


def _gate_weights(w_a, w_i):
    L, nb, bw, _ = w_a.shape

    def bd(w):
        w = w.reshape(L, nb // 2, 2, bw, bw)
        z = jnp.zeros_like(w[:, :, 0])
        top = jnp.concatenate([w[:, :, 0], z], axis=-1)
        bot = jnp.concatenate([z, w[:, :, 1]], axis=-1)
        return jnp.concatenate([top, bot], axis=-2)

    return jnp.concatenate([bd(w_a), bd(w_i)], axis=-1).astype(BF16)


def kernel(x, norm_mix_pre, norm_mix_post, w_in, attn_sinks, w_attn_br, conv_w, conv_b,
           w_rg_a, b_rg_a, w_rg_i, b_rg_i, lru_lambda, w_rnn_br, w_pool_groups, pool_scale,
           w_pool_br, w_out, norm_mlp_pre, norm_mlp_post, w_mlp_up, w_mlp_down):
    B, S, D = x.shape
    depth = w_in.shape[0]
    assert S % MIX_TILE == 0 and S % MLP_TILE == 0 and MIX_TILE % ATT_BLOCK == 0
    tab = _rope_tables(S)
    wg = _gate_weights(w_rg_a, w_rg_i)
    row = lambda a, l: a[l][None, :]
    h = x
    for l in range(depth):
        h = _mixer_call(
            h, tab, row(norm_mix_pre, l), row(norm_mix_post, l), w_in[l].astype(BF16),
            attn_sinks[l], w_attn_br[l].astype(BF16), conv_w[l], row(conv_b, l), wg[l],
            row(b_rg_a, l), row(b_rg_i, l), row(lru_lambda, l), w_rnn_br[l].astype(BF16),
            w_pool_groups[l].astype(BF16), row(pool_scale, l), w_pool_br[l].astype(BF16),
            w_out[l].astype(BF16))
        h = _mlp_call(h, row(norm_mlp_pre, l), row(norm_mlp_post, l),
                      w_mlp_up[l].astype(BF16), w_mlp_down[l].astype(BF16))
    return h
```

```python
import functools

import jax
import jax.numpy as jnp
from jax import lax
from jax.experimental import pallas as pl
from jax.experimental.pallas import tpu as pltpu

F32 = jnp.float32
BF16 = jnp.bfloat16

HEAD_DIM = 64
N_Q_HEADS = 16
N_KV_HEADS = 4
ATT_BLOCK = 128
ROT_DIM = HEAD_DIM // 4
ROPE_THETA = 500000.0
RNN_BW = 64
CONV_WIDTH = 4
LRU_C = 8.0
POOL_WINDOWS = (2, 4, 8, 16)
EPS = 1e-6
NEG_INF = -1e30

LANES = 128
SUBLANES = 8
CONV_CARRY = SUBLANES
POOL_CARRY = 32
VMEM_LIMIT_BYTES = 58 * 1024 * 1024

MIX_TILE = 256
MLP_TILE = 512
FF_CHUNK = 1024


def _dot(a, b):
    return jnp.dot(a, b, preferred_element_type=F32)


def _dot_nt(a, b):
    return lax.dot_general(a, b, (((1,), (1,)), ((), ())), preferred_element_type=F32)


def _rms(x, g):
    ms = jnp.mean(x * x, axis=-1, keepdims=True)
    return (x * lax.rsqrt(ms + EPS)) * g


def _rope(x, c, s1, s2):
    half = ROT_DIM // 2
    return x * c + pltpu.roll(x, LANES - half, 1) * s1 + pltpu.roll(x, half, 1) * s2


def _after(x, probe):
    sh = jnp.uint32(16)
    z = lax.shift_right_logical(lax.shift_right_logical(pltpu.bitcast(probe, jnp.uint32), sh), sh)
    return pltpu.bitcast(pltpu.bitcast(x, jnp.uint32) | z, F32)


def _head_variants(x, kc):
    lane = lax.broadcasted_iota(jnp.int32, x.shape, 1)
    lo = lane < HEAD_DIM
    xs = pltpu.roll(x, HEAD_DIM, 1)
    zero = jnp.zeros_like(x)
    return {
        (2 * kc, 0): jnp.where(lo, x, zero),
        (2 * kc, 1): jnp.where(lo, zero, xs),
        (2 * kc + 1, 0): jnp.where(lo, xs, zero),
        (2 * kc + 1, 1): jnp.where(lo, zero, x),
    }


def _mixer_kernel(h_ref, tab_ref, gpre_ref, gpost_ref, win_ref, sink_ref, wattn_ref,
                  convw_ref, convb_ref, wg_ref, ba_ref, bi_ref, lam_ref, wrnn_ref,
                  wpg_ref, psc_ref, wpool_ref, wout_ref, o_ref,
                  u_s, q_s, k_s, v_s, ao_s, xb_s, xc_s, ga_s, gi_s, y_s, hst_s, ro_s,
                  pb_s, pa_s, pc_s, po_s, acc_s, g0_s, g1_s, g2_s, p2_s, mg_s,
                  sc_s, pw_s, pd_s):
    si = pl.program_id(1)
    T, D = h_ref.shape
    nblk = T // ATT_BLOCK

    @pl.when(si == 0)
    def _():
        k_s[:, 0:ATT_BLOCK, :] = jnp.zeros((2 * N_KV_HEADS, ATT_BLOCK, LANES), BF16)
        v_s[:, 0:ATT_BLOCK, :] = jnp.zeros((2 * N_KV_HEADS, ATT_BLOCK, LANES), BF16)
        xb_s[0:CONV_CARRY, :] = jnp.zeros((CONV_CARRY, D), F32)
        pb_s[0:POOL_CARRY, :] = jnp.zeros((POOL_CARRY, D), F32)
        hst_s[...] = jnp.zeros(hst_s.shape, F32)

    u_s[...] = _rms(h_ref[...], gpre_ref[...]).astype(BF16)

    kvw = N_KV_HEADS * HEAD_DIM
    base = D + 2 * kvw
    gate0 = base + 3 * D
    P = POOL_CARRY
    CW = D // len(POOL_WINDOWS)
    nchunk = D // CW
    tasks = {}
    pending = []

    def probe_of(dst, row0, c):
        pending[:] = [lambda: dst[row0:row0 + SUBLANES, cols(c)]]

    def task(name, deps=(), m=0, v=0):
        def reg(fn):
            tasks[name] = (fn, tuple(deps), m, v)
            return fn
        return reg

    def cols(c):
        return slice(c * CW, (c + 1) * CW)

    def proj(col0, width=CW):
        return _dot(u_s[...], win_ref[:, col0:col0 + width])

    c_q, s1_q, s2_q = tab_ref[0], tab_ref[1], tab_ref[2]
    inv_scale = float(HEAD_DIM) ** 0.5
    cur = slice(ATT_BLOCK, ATT_BLOCK + T)

    @task("kv", m=128, v=150)
    def _():
        c_k, s1_k, s2_k = c_q * inv_scale, s1_q * inv_scale, s2_q * inv_scale
        k = proj(D, kvw)
        v = proj(D + kvw, kvw)
        for kc in range(kvw // LANES):
            sl = slice(kc * LANES, (kc + 1) * LANES)
            for (head, var), val in _head_variants(_rope(k[:, sl], c_k, s1_k, s2_k), kc).items():
                k_s[2 * head + var, cur, :] = val.astype(BF16)
            for (head, var), val in _head_variants(v[:, sl], kc).items():
                v_s[2 * head + var, cur, :] = val.astype(BF16)

    def make_q(j):
        @task(f"q{j}", m=256, v=100)
        def _():
            q = proj(2 * j * LANES, 2 * LANES)
            for c in range(2):
                sl = slice((2 * j + c) * LANES, (2 * j + c + 1) * LANES)
                q_s[:, sl] = _rope(q[:, c * LANES:(c + 1) * LANES], c_q, s1_q, s2_q).astype(BF16)

    row = lax.broadcasted_iota(jnp.int32, (2 * ATT_BLOCK, ATT_BLOCK), 0)
    col = lax.broadcasted_iota(jnp.int32, (2 * ATT_BLOCK, ATT_BLOCK), 1)
    qpos = jnp.where(row >= ATT_BLOCK, row - ATT_BLOCK, row)
    from_prev = col > qpos
    top = row < ATT_BLOCK

    def make_attn(b, j):
        r0 = b * ATT_BLOCK
        slot = (j * nblk + b) % 2

        @task(f"qk{b}_{j}", deps=("kv", f"q{j}"), m=128, v=40)
        def _():
            has_prev = (si * nblk + b) > 0
            qst = jnp.concatenate(
                [q_s[r0:r0 + ATT_BLOCK, (2 * j) * LANES:(2 * j + 1) * LANES],
                 q_s[r0:r0 + ATT_BLOCK, (2 * j + 1) * LANES:(2 * j + 2) * LANES]], axis=0)
            for var in range(2):
                kcat = k_s[2 * j + var, r0:r0 + 2 * ATT_BLOCK, :]
                s = _dot_nt(qst, kcat)
                s_prev = jnp.where(has_prev, s[:, 0:ATT_BLOCK], NEG_INF)
                sc_s[slot, var] = jnp.where(from_prev, s_prev, s[:, ATT_BLOCK:])

        @task(f"sm{b}_{j}", deps=(f"qk{b}_{j}",), v=220)
        def _():
            for var in range(2):
                sc = sc_s[slot, var]
                sink = jnp.where(top, sink_ref[4 * j + var], sink_ref[4 * j + 2 + var])[:, 0:1]
                m = jnp.maximum(jnp.max(sc, axis=-1, keepdims=True), sink)
                p = jnp.exp(sc - m)
                denom = jnp.sum(p, axis=-1, keepdims=True) + jnp.exp(sink - m)
                p = p * (1.0 / denom)
                zero = jnp.zeros_like(p)
                pw_s[slot, var] = jnp.concatenate(
                    [jnp.where(from_prev, p, zero), jnp.where(from_prev, zero, p)],
                    axis=1).astype(BF16)

        @task(f"pv{b}_{j}", deps=(f"sm{b}_{j}",), m=128, v=20)
        def _():
            o = None
            for var in range(2):
                vcat = v_s[2 * j + var, r0:r0 + 2 * ATT_BLOCK, :]
                ov = _dot(pw_s[slot, var], vcat)
                o = ov if o is None else o + ov
            ao_s[r0:r0 + ATT_BLOCK, (2 * j) * LANES:(2 * j + 1) * LANES] = (
                o[0:ATT_BLOCK].astype(BF16))
            ao_s[r0:r0 + ATT_BLOCK, (2 * j + 1) * LANES:(2 * j + 2) * LANES] = (
                o[ATT_BLOCK:].astype(BF16))

    units = [(b, j) for j in range(N_KV_HEADS) for b in range(nblk)]
    for j in range(N_KV_HEADS):
        make_q(j)
    for b, j in units:
        make_attn(b, j)
    all_att = tuple(f"pv{b}_{j}" for b, j in units)

    def make_proj(name, dst, row0, col0, c):
        @task(f"{name}{c}", m=256)
        def _():
            dst[row0:row0 + T, cols(c)] = proj(col0 + c * CW)
            probe_of(dst, row0, c)

    def make_gate(name, dst, idx, c):
        @task(f"{name}{c}", m=256, v=60)
        def _():
            dst[:, cols(c)] = jax.nn.sigmoid(proj(gate0 + idx * D + c * CW))
            probe_of(dst, 0, c)

    for c in range(nchunk):
        make_proj("xr", xb_s, CONV_CARRY, base, c)
        make_proj("yr", y_s, 0, base + D, c)
        make_proj("pp", pb_s, P, base + 2 * D, c)
        make_gate("g0_", g0_s, 0, c)
        make_gate("g1_", g1_s, 1, c)
        make_gate("g2_", g2_s, 2, c)

    def make_conv(c):
        @task(f"conv{c}", deps=(f"xr{c}",), v=330)
        def _():
            sl = cols(c)
            xc = convb_ref[:, sl]
            for tap in range(CONV_WIDTH):
                off = CONV_CARRY - (CONV_WIDTH - 1) + tap
                xc = xc + xb_s[off:off + T, sl] * convw_ref[tap:tap + 1, sl]
            xc_s[:, sl] = xc
            xb_s[0:CONV_CARRY, sl] = xb_s[T:T + CONV_CARRY, sl]

        @task(f"gm{c}", deps=(f"conv{c}",), m=64, v=16)
        def _():
            for lc in range(c * CW // LANES, (c + 1) * CW // LANES):
                ls = slice(lc * LANES, (lc + 1) * LANES)
                res = _dot(xc_s[:, ls].astype(BF16), wg_ref[lc])
                ga_s[:, ls] = res[:, 0:LANES]
                gi_s[:, ls] = res[:, LANES:]

    def make_pool(g):
        sl = cols(g)
        w = POOL_WINDOWS[g]

        @task(f"pool{g}", deps=(f"pp{g}",), v=180)
        def _():
            src, dst, lvl = pb_s, pa_s, 1
            while 2 * lvl < w:
                r = 8 * ((lvl).bit_length())
                dst[r:P + T, sl] = src[r:P + T, sl] + src[r - lvl:P + T - lvl, sl]
                src, dst = dst, (pc_s if dst is pa_s else pa_s)
                lvl *= 2
            ssum = src[P:P + T, sl] + src[P - lvl:P + T - lvl, sl]
            tpos = (si * T + lax.broadcasted_iota(jnp.int32, (T, 1), 0) + 1).astype(F32)
            inv_cnt = 1.0 / jnp.minimum(tpos, float(w))
            pd_s[:, sl] = (ssum * inv_cnt - pb_s[P:P + T, sl]).astype(BF16)
            pb_s[0:P, sl] = pb_s[T:T + P, sl]

        @task(f"pm{g}", deps=(f"pool{g}",), m=64, v=16)
        def _():
            mixed = _dot(pd_s[:, sl], wpg_ref[g])
            po_s[:, sl] = (mixed * psc_ref[:, sl]).astype(BF16)

    for c in range(nchunk):
        make_conv(c)
        make_pool(c)
    all_pool = tuple(f"pm{g}" for g in range(nchunk))

    lam = lam_ref[...]
    ls8 = LRU_C * (jnp.minimum(lam, 0.0) - jnp.log1p(jnp.exp(-jnp.abs(lam))))
    rowi = lax.broadcasted_iota(jnp.int32, (SUBLANES, CW), 0)
    hcar = {}
    nstep = T // (2 * SUBLANES)

    def scan_rows(r0, sl, hprev):
        rows = slice(r0, r0 + SUBLANES)
        r = jax.nn.sigmoid(ga_s[rows, sl] + ba_ref[:, sl])
        i = jax.nn.sigmoid(gi_s[rows, sl] + bi_ref[:, sl])
        log_a = r * ls8[:, sl]
        a = jnp.exp(log_a)
        y = jnp.tanh(-log_a) * (a * a + 1.0)
        mult = jnp.where(y > 0.0, y * lax.rsqrt(y), 0.0)
        uu = mult * (i * xc_s[rows, sl])
        for kk in (1, 2, 4):
            keep = rowi >= kk
            a_sh = jnp.where(keep, pltpu.roll(a, kk, 0), 1.0)
            u_sh = jnp.where(keep, pltpu.roll(uu, kk, 0), 0.0)
            uu = a * u_sh + uu
            a = a * a_sh
        hrows = a * hprev + uu
        hlast = jnp.broadcast_to(hrows[SUBLANES - 1:SUBLANES, :], (SUBLANES, CW))
        return hrows * jax.nn.gelu(y_s[rows, sl]), hlast

    def make_scan(c, n):
        deps = (f"gm{c}", f"yr{c}") if n == 0 else (f"scan{c}_{n - 1}",)

        @task(f"scan{c}_{n}", deps=deps, v=70)
        def _():
            sl = cols(c)
            r0 = n * 2 * SUBLANES
            h = hst_s[:, sl] if n == 0 else hcar[c]
            if pending:
                h = _after(h, pending.pop()())
            out0, h = scan_rows(r0, sl, h)
            out1, h = scan_rows(r0 + SUBLANES, sl, h)
            ro_s[r0:r0 + 2 * SUBLANES, sl] = jnp.concatenate([out0, out1], axis=0).astype(BF16)
            if n == nstep - 1:
                hst_s[:, sl] = h
            else:
                hcar[c] = h

    for c in range(nchunk):
        for n in range(nstep):
            make_scan(c, n)
    all_scan = tuple(f"scan{c}_{nstep - 1}" for c in range(nchunk))

    def make_branch(name, dst, gate, gname, src, w_ref, deps, c):
        @task(f"{name}{c}", deps=deps + (f"{gname}{c}",), m=256, v=16)
        def _():
            dst[:, cols(c)] = gate[:, cols(c)] * _dot(src[...], w_ref[:, cols(c)])
            probe_of(dst, 0, c)

    def make_merge(c):
        @task(f"merge{c}", deps=all_scan + (f"g1_{c}", f"abr{c}", f"pbr{c}"), m=256, v=50)
        def _():
            sl = cols(c)
            rnn = g1_s[:, sl] * _dot(ro_s[...], wrnn_ref[:, sl])
            mg_s[:, sl] = ((acc_s[:, sl] + rnn) + p2_s[:, sl]).astype(BF16)

    for c in range(nchunk):
        make_branch("abr", acc_s, g0_s, "g0_", ao_s, wattn_ref, all_att, c)
        make_branch("pbr", p2_s, g2_s, "g2_", po_s, wpool_ref, all_pool, c)
        make_merge(c)

    MXU_GAP = 384
    fillers = [f"xr{0}"]
    for c in range(1, nchunk):
        fillers += [f"q{c}", f"xr{c}", f"yr{c - 1}"]
    fillers += [f"yr{nchunk - 1}"] + [f"pp{c}" for c in range(nchunk)]
    fillers += [f"g{i}_{c}" for i in (0, 2, 1) for c in range(nchunk)]
    main = ["kv", "q0"]
    for n, (b, j) in enumerate(units):
        main.append(f"qk{b}_{j}")
        main += fillers[n * len(fillers) // len(units):(n + 1) * len(fillers) // len(units)]
    main += [n for c in range(nchunk) for n in (f"abr{c}", f"pbr{c}")]
    n_before_merge = len(main)
    main += [f"merge{c}" for c in range(nchunk)]
    follow = {f"qk{b}_{j}": (f"sm{b}_{j}", f"pv{b}_{j}") for b, j in units}
    follow.update({f"xr{c}": (f"conv{c}", f"gm{c}") for c in range(nchunk)})
    follow.update({f"pp{c}": (f"pool{c}", f"pm{c}") for c in range(nchunk)})
    scan_next = [0] * nchunk
    done = []
    spent = [0]
    deferred = []

    def emit(name):
        if name in done:
            return
        fn, deps, m, v = tasks[name]
        for d in deps:
            emit(d)
        fn()
        done.append(name)
        spent[0] += m

    def scans_left():
        return sum(nstep - s for s in scan_next)

    for pos, name in enumerate(main):
        emit(name)
        for item in list(deferred):
            if spent[0] >= item[0]:
                emit(item[1])
                deferred.remove(item)
        if name in follow:
            producer, consumer = follow[name]
            emit(producer)
            deferred.append((spent[0] + MXU_GAP, consumer))
        quota = -(-scans_left() // max(1, n_before_merge - pos))
        for c in sorted(range(nchunk), key=lambda c: scan_next[c]):
            if quota > 0 and scan_next[c] < nstep and f"gm{c}" in done and f"yr{c}" in done:
                emit(f"scan{c}_{scan_next[c]}")
                scan_next[c] += 1
                quota -= 1
    assert set(done) == set(tasks), set(tasks) ^ set(done)

    k_s[:, 0:ATT_BLOCK, :] = k_s[:, T:T + ATT_BLOCK, :]
    v_s[:, 0:ATT_BLOCK, :] = v_s[:, T:T + ATT_BLOCK, :]

    mix = _dot(mg_s[...], wout_ref[...])
    o_ref[...] = h_ref[...] + _rms(mix, gpost_ref[...])


def _mlp_kernel(h_ref, gpre_ref, gpost_ref, wup_ref, wdown_ref, o_ref, m_s, acc_s):
    d_ff = wup_ref.shape[1]
    m_s[...] = _rms(h_ref[...], gpre_ref[...]).astype(BF16)
    for c in range(d_ff // FF_CHUNK):
        sl = slice(c * FF_CHUNK, (c + 1) * FF_CHUNK)
        up = jnp.maximum(_dot(m_s[...], wup_ref[:, sl]), 0.0)
        part = _dot((up * up).astype(BF16), wdown_ref[sl, :])
        if c == 0:
            acc_s[...] = part
        else:
            acc_s[...] = acc_s[...] + part
    o_ref[...] = h_ref[...] + _rms(acc_s[...], gpost_ref[...])


def _const_spec(shape):
    nd = len(shape)
    return pl.BlockSpec(shape, lambda b, s: (0,) * nd, pipeline_mode=pl.Buffered(1))


def _mixer_call(h, tab, gpre, gpost, win, sinks, wattn, convw, convb, wg, ba, bi, lam,
                wrnn, wpg, psc, wpool, wout):
    B, S, D = h.shape
    T = MIX_TILE
    tile = pl.BlockSpec((None, T, D), lambda b, s: (b, s, 0))
    in_specs = [
        tile,
        pl.BlockSpec((3, T, LANES), lambda b, s: (0, s, 0)),
        _const_spec(gpre.shape), _const_spec(gpost.shape), _const_spec(win.shape),
        pl.BlockSpec(memory_space=pltpu.SMEM),
        _const_spec(wattn.shape), _const_spec(convw.shape), _const_spec(convb.shape),
        _const_spec(wg.shape), _const_spec(ba.shape), _const_spec(bi.shape),
        _const_spec(lam.shape), _const_spec(wrnn.shape), _const_spec(wpg.shape),
        _const_spec(psc.shape), _const_spec(wpool.shape), _const_spec(wout.shape),
    ]
    nvar = 2 * N_KV_HEADS
    scratch = [
        pltpu.VMEM((T, D), BF16),
        pltpu.VMEM((T, D), BF16),
        pltpu.VMEM((nvar, ATT_BLOCK + T, LANES), BF16),
        pltpu.VMEM((nvar, ATT_BLOCK + T, LANES), BF16),
        pltpu.VMEM((T, D), BF16),
        pltpu.VMEM((CONV_CARRY + T, D), F32),
        pltpu.VMEM((T, D), F32),
        pltpu.VMEM((T, D), F32),
        pltpu.VMEM((T, D), F32),
        pltpu.VMEM((T, D), F32),
        pltpu.VMEM((SUBLANES, D), F32),
        pltpu.VMEM((T, D), BF16),
        pltpu.VMEM((POOL_CARRY + T, D), F32),
        pltpu.VMEM((POOL_CARRY + T, D), F32),
        pltpu.VMEM((POOL_CARRY + T, D), F32),
        pltpu.VMEM((T, D), BF16),
        pltpu.VMEM((T, D), F32),
        pltpu.VMEM((T, D), F32),
        pltpu.VMEM((T, D), F32),
        pltpu.VMEM((T, D), F32),
        pltpu.VMEM((T, D), F32),
        pltpu.VMEM((T, D), BF16),
        pltpu.VMEM((2, 2, 2 * ATT_BLOCK, ATT_BLOCK), F32),
        pltpu.VMEM((2, 2, 2 * ATT_BLOCK, 2 * ATT_BLOCK), BF16),
        pltpu.VMEM((T, D), BF16),
    ]
    return pl.pallas_call(
        _mixer_kernel,
        grid=(B, S // T),
        in_specs=in_specs,
        out_specs=tile,
        out_shape=jax.ShapeDtypeStruct(h.shape, h.dtype),
        scratch_shapes=scratch,
        compiler_params=pltpu.CompilerParams(
            dimension_semantics=("arbitrary", "arbitrary"),
            vmem_limit_bytes=VMEM_LIMIT_BYTES),
        name="mixer",
    )(h, tab, gpre, gpost, win, sinks, wattn, convw, convb, wg, ba, bi, lam, wrnn, wpg,
      psc, wpool, wout)


def _mlp_call(h, gpre, gpost, wup, wdown):
    B, S, D = h.shape
    T = MLP_TILE
    tile = pl.BlockSpec((None, T, D), lambda b, s: (b, s, 0))
    return pl.pallas_call(
        _mlp_kernel,
        grid=(B, S // T),
        in_specs=[tile, _const_spec(gpre.shape), _const_spec(gpost.shape),
                  _const_spec(wup.shape), _const_spec(wdown.shape)],
        out_specs=tile,
        out_shape=jax.ShapeDtypeStruct(h.shape, h.dtype),
        scratch_shapes=[pltpu.VMEM((T, D), BF16), pltpu.VMEM((T, D), F32)],
        compiler_params=pltpu.CompilerParams(
            dimension_semantics=("arbitrary", "arbitrary"),
            vmem_limit_bytes=VMEM_LIMIT_BYTES),
        name="mlp",
    )(h, gpre, gpost, wup, wdown)


def _rope_tables(seq):
    half = ROT_DIM // 2
    inv_freq = ROPE_THETA ** (-jnp.arange(0, ROT_DIM, 2, dtype=F32) / ROT_DIM)
    ang = jnp.arange(seq, dtype=F32)[:, None] * inv_freq[None, :]
    cos, sin = jnp.cos(ang), jnp.sin(ang)
    pad = jnp.zeros((seq, HEAD_DIM - ROT_DIM), F32)
    c = jnp.concatenate([cos, cos, pad + 1.0], axis=1)
    s1 = jnp.concatenate([-sin, jnp.zeros_like(sin), pad], axis=1)
    s2 = jnp.concatenate([jnp.zeros_like(sin), sin, pad], axis=1)
    tab = jnp.stack([c, s1, s2])
    tab = jnp.concatenate([tab] * (LANES // HEAD_DIM), axis=2)
    return tab * (float(HEAD_DIM) ** -0.5)


def _gate_weights(w_a, w_i):
    L, nb, bw, _ = w_a.shape

    def bd(w):
        w = w.reshape(L, nb // 2, 2, bw, bw)
        z = jnp.zeros_like(w[:, :, 0])
        top = jnp.concatenate([w[:, :, 0], z], axis=-1)
        bot = jnp.concatenate([z, w[:, :, 1]], axis=-1)
        return jnp.concatenate([top, bot], axis=-2)

    return jnp.concatenate([bd(w_a), bd(w_i)], axis=-1).astype(BF16)


def kernel(x, norm_mix_pre, norm_mix_post, w_in, attn_sinks, w_attn_br, conv_w, conv_b,
           w_rg_a, b_rg_a, w_rg_i, b_rg_i, lru_lambda, w_rnn_br, w_pool_groups, pool_scale,
           w_pool_br, w_out, norm_mlp_pre, norm_mlp_post, w_mlp_up, w_mlp_down):
    B, S, D = x.shape
    depth = w_in.shape[0]
    assert S % MIX_TILE == 0 and S % MLP_TILE == 0 and MIX_TILE % ATT_BLOCK == 0
    tab = _rope_tables(S)
    wg = _gate_weights(w_rg_a, w_rg_i)
    row = lambda a, l: a[l][None, :]
    h = x
    for l in range(depth):
        h = _mixer_call(
            h, tab, row(norm_mix_pre, l), row(norm_mix_post, l), w_in[l].astype(BF16),
            attn_sinks[l], w_attn_br[l].astype(BF16), conv_w[l], row(conv_b, l), wg[l],
            row(b_rg_a, l), row(b_rg_i, l), row(lru_lambda, l), w_rnn_br[l].astype(BF16),
            w_pool_groups[l].astype(BF16), row(pool_scale, l), w_pool_br[l].astype(BF16),
            w_out[l].astype(BF16))
        h = _mlp_call(h, row(norm_mlp_pre, l), row(norm_mlp_post, l),
                      w_mlp_up[l].astype(BF16), w_mlp_down[l].astype(BF16))
    return h
```

```python
import functools

import jax
import jax.numpy as jnp
from jax import lax
from jax.experimental import pallas as pl
from jax.experimental.pallas import tpu as pltpu

F32 = jnp.float32
BF16 = jnp.bfloat16

HEAD_DIM = 64
N_Q_HEADS = 16
N_KV_HEADS = 4
ATT_BLOCK = 128
ROT_DIM = HEAD_DIM // 4
ROPE_THETA = 500000.0
RNN_BW = 64
CONV_WIDTH = 4
LRU_C = 8.0
POOL_WINDOWS = (2, 4, 8, 16)
EPS = 1e-6
NEG_INF = -1e30

LANES = 128
SUBLANES = 8
CONV_CARRY = SUBLANES
POOL_CARRY = 32
VMEM_LIMIT_BYTES = 58 * 1024 * 1024

MIX_TILE = 256
MLP_TILE = 512
FF_CHUNK = 1024


def _dot(a, b):
    return jnp.dot(a, b, preferred_element_type=F32)


def _dot_nt(a, b):
    return lax.dot_general(a, b, (((1,), (1,)), ((), ())), preferred_element_type=F32)


def _rms(x, g):
    ms = jnp.mean(x * x, axis=-1, keepdims=True)
    return (x * lax.rsqrt(ms + EPS)) * g


def _rope(x, c, s1, s2):
    half = ROT_DIM // 2
    return x * c + pltpu.roll(x, LANES - half, 1) * s1 + pltpu.roll(x, half, 1) * s2


def _after(x, probe):
    sh = jnp.uint32(16)
    z = lax.shift_right_logical(lax.shift_right_logical(pltpu.bitcast(probe, jnp.uint32), sh), sh)
    return pltpu.bitcast(pltpu.bitcast(x, jnp.uint32) | z, F32)


def _after_all(x, y):
    acc = y[0:SUBLANES, :]
    for r in range(SUBLANES, y.shape[0], SUBLANES):
        acc = acc + y[r:r + SUBLANES, :]
    blk = acc[:, 0:LANES]
    for c in range(LANES, y.shape[1], LANES):
        blk = blk + acc[:, c:c + LANES]
    top = _after(x[0:SUBLANES, 0:LANES], blk)
    if x.shape[1] > LANES:
        top = jnp.concatenate([top, x[0:SUBLANES, LANES:]], axis=1)
    if x.shape[0] > SUBLANES:
        top = jnp.concatenate([top, x[SUBLANES:, :]], axis=0)
    return top


def _head_variants(x, kc):
    lane = lax.broadcasted_iota(jnp.int32, x.shape, 1)
    lo = lane < HEAD_DIM
    xs = pltpu.roll(x, HEAD_DIM, 1)
    zero = jnp.zeros_like(x)
    return {
        (2 * kc, 0): jnp.where(lo, x, zero),
        (2 * kc, 1): jnp.where(lo, zero, xs),
        (2 * kc + 1, 0): jnp.where(lo, xs, zero),
        (2 * kc + 1, 1): jnp.where(lo, zero, x),
    }


def _mixer_kernel(seq_tiles, h_ref, hprev_ref, hnext_ref, tab_ref, gpre_ref, gpost_ref,
                  win_ref, sink_ref, wattn_ref,
                  convw_ref, convb_ref, wg_ref, ba_ref, bi_ref, lam_ref, wrnn_ref,
                  wpg_ref, psc_ref, wpool_ref, wout_ref, o_ref,
                  u_s, q_s, k_s, v_s, ao_s, xb_s, xc_s, ga_s, gi_s, y_s, hst_s, ro_s,
                  pb_s, pa_s, pc_s, po_s, acc_s, g0_s, g1_s, g2_s, p2_s, mg_s,
                  sc_s, pw_s, pd_s, mix_s):
    g = pl.program_id(0)
    si = lax.rem(g, seq_tiles)
    T, D = h_ref.shape
    nblk = T // ATT_BLOCK

    @pl.when(g == 0)
    def _():
        u_s[...] = _rms(h_ref[...], gpre_ref[...]).astype(BF16)
        mg_s[...] = jnp.zeros(mg_s.shape, BF16)

    @pl.when(si == 0)
    def _():
        k_s[:, 0:ATT_BLOCK, :] = jnp.zeros((2 * N_KV_HEADS, ATT_BLOCK, LANES), BF16)
        v_s[:, 0:ATT_BLOCK, :] = jnp.zeros((2 * N_KV_HEADS, ATT_BLOCK, LANES), BF16)
        xb_s[0:CONV_CARRY, :] = jnp.zeros((CONV_CARRY, D), F32)
        pb_s[0:POOL_CARRY, :] = jnp.zeros((POOL_CARRY, D), F32)
        hst_s[...] = jnp.zeros(hst_s.shape, F32)

    kvw = N_KV_HEADS * HEAD_DIM
    base = D + 2 * kvw
    gate0 = base + 3 * D
    P = POOL_CARRY
    CW = D // len(POOL_WINDOWS)
    nchunk = D // CW
    tasks = {}
    anchor = {}
    pending = []

    def probe_of(dst, row0, c):
        pending[:] = [lambda: dst[row0:row0 + SUBLANES, cols(c)]]

    def task(name, deps=(), m=0, v=0):
        def reg(fn):
            tasks[name] = (fn, tuple(deps), m, v)
            return fn
        return reg

    def cols(c):
        return slice(c * CW, (c + 1) * CW)

    def proj(col0, width=CW):
        return _dot(u_s[...], win_ref[:, col0:col0 + width])

    c_q, s1_q, s2_q = tab_ref[0], tab_ref[1], tab_ref[2]
    inv_scale = float(HEAD_DIM) ** 0.5
    cur = slice(ATT_BLOCK, ATT_BLOCK + T)

    @task("kv", m=128, v=150)
    def _():
        c_k, s1_k, s2_k = c_q * inv_scale, s1_q * inv_scale, s2_q * inv_scale
        k = proj(D, kvw)
        v = proj(D + kvw, kvw)
        for kc in range(kvw // LANES):
            sl = slice(kc * LANES, (kc + 1) * LANES)
            for (head, var), val in _head_variants(_rope(k[:, sl], c_k, s1_k, s2_k), kc).items():
                k_s[2 * head + var, cur, :] = val.astype(BF16)
            for (head, var), val in _head_variants(v[:, sl], kc).items():
                v_s[2 * head + var, cur, :] = val.astype(BF16)

    def make_q(j):
        @task(f"q{j}", m=256, v=100)
        def _():
            q = proj(2 * j * LANES, 2 * LANES)
            for c in range(2):
                sl = slice((2 * j + c) * LANES, (2 * j + c + 1) * LANES)
                q_s[:, sl] = _rope(q[:, c * LANES:(c + 1) * LANES], c_q, s1_q, s2_q).astype(BF16)

    row = lax.broadcasted_iota(jnp.int32, (2 * ATT_BLOCK, ATT_BLOCK), 0)
    col = lax.broadcasted_iota(jnp.int32, (2 * ATT_BLOCK, ATT_BLOCK), 1)
    qpos = jnp.where(row >= ATT_BLOCK, row - ATT_BLOCK, row)
    from_prev = col > qpos
    top = row < ATT_BLOCK

    def make_attn(b, j):
        r0 = b * ATT_BLOCK
        slot = (j * nblk + b) % 2

        @task(f"qk{b}_{j}", deps=("kv", f"q{j}"), m=128, v=40)
        def _():
            has_prev = (si * nblk + b) > 0
            qst = jnp.concatenate(
                [q_s[r0:r0 + ATT_BLOCK, (2 * j) * LANES:(2 * j + 1) * LANES],
                 q_s[r0:r0 + ATT_BLOCK, (2 * j + 1) * LANES:(2 * j + 2) * LANES]], axis=0)
            for var in range(2):
                kcat = k_s[2 * j + var, r0:r0 + 2 * ATT_BLOCK, :]
                s = _dot_nt(qst, kcat)
                s_prev = jnp.where(has_prev, s[:, 0:ATT_BLOCK], NEG_INF)
                sc_s[slot, var] = jnp.where(from_prev, s_prev, s[:, ATT_BLOCK:])

        @task(f"sm{b}_{j}", deps=(f"qk{b}_{j}",), v=220)
        def _():
            for var in range(2):
                sc = sc_s[slot, var]
                sink = jnp.where(top, sink_ref[4 * j + var], sink_ref[4 * j + 2 + var])[:, 0:1]
                m = jnp.maximum(jnp.max(sc, axis=-1, keepdims=True), sink)
                p = jnp.exp(sc - m)
                denom = jnp.sum(p, axis=-1, keepdims=True) + jnp.exp(sink - m)
                p = p * (1.0 / denom)
                zero = jnp.zeros_like(p)
                pw_s[slot, var] = jnp.concatenate(
                    [jnp.where(from_prev, p, zero), jnp.where(from_prev, zero, p)],
                    axis=1).astype(BF16)

        @task(f"pv{b}_{j}", deps=(f"sm{b}_{j}",), m=128, v=20)
        def _():
            o = None
            for var in range(2):
                vcat = v_s[2 * j + var, r0:r0 + 2 * ATT_BLOCK, :]
                ov = _dot(pw_s[slot, var], vcat)
                o = ov if o is None else o + ov
            ao_s[r0:r0 + ATT_BLOCK, (2 * j) * LANES:(2 * j + 1) * LANES] = (
                o[0:ATT_BLOCK].astype(BF16))
            ao_s[r0:r0 + ATT_BLOCK, (2 * j + 1) * LANES:(2 * j + 2) * LANES] = (
                o[ATT_BLOCK:].astype(BF16))

    units = [(b, j) for j in range(N_KV_HEADS) for b in range(nblk)]
    for j in range(N_KV_HEADS):
        make_q(j)
    for b, j in units:
        make_attn(b, j)
    all_att = tuple(f"pv{b}_{j}" for b, j in units)

    def make_proj(name, dst, row0, col0, c):
        @task(f"{name}{c}", m=256)
        def _():
            dst[row0:row0 + T, cols(c)] = proj(col0 + c * CW)
            probe_of(dst, row0, c)

    def make_gate(name, dst, idx, c):
        @task(f"{name}{c}", m=256, v=60)
        def _():
            dst[:, cols(c)] = jax.nn.sigmoid(proj(gate0 + idx * D + c * CW))
            probe_of(dst, 0, c)

    for c in range(nchunk):
        make_proj("xr", xb_s, CONV_CARRY, base, c)
        make_proj("yr", y_s, 0, base + D, c)
        make_proj("pp", pb_s, P, base + 2 * D, c)
        make_gate("g0_", g0_s, 0, c)
        make_gate("g1_", g1_s, 1, c)
        make_gate("g2_", g2_s, 2, c)

    def make_conv(c):
        @task(f"conv{c}", deps=(f"xr{c}",), v=330)
        def _():
            sl = cols(c)
            xall = xb_s[:, sl]
            xc = convb_ref[:, sl]
            for tap in range(CONV_WIDTH):
                lag = CONV_WIDTH - 1 - tap
                xlag = pltpu.roll(xall, lag, 0) if lag else xall
                xc = xc + xlag[CONV_CARRY:, :] * convw_ref[tap:tap + 1, sl]
            xc_s[:, sl] = xc
            xb_s[0:CONV_CARRY, sl] = xb_s[T:T + CONV_CARRY, sl]

        @task(f"gm{c}", deps=(f"conv{c}",), m=64, v=16)
        def _():
            for lc in range(c * CW // LANES, (c + 1) * CW // LANES):
                ls = slice(lc * LANES, (lc + 1) * LANES)
                res = _dot(xc_s[:, ls].astype(BF16), wg_ref[lc])
                ga_s[:, ls] = res[:, 0:LANES]
                gi_s[:, ls] = res[:, LANES:]

    def make_pool(g):
        sl = cols(g)
        w = POOL_WINDOWS[g]

        @task(f"pool{g}", deps=(f"pp{g}",), v=180)
        def _():
            src, dst, lvl = pb_s, pa_s, 1
            while 2 * lvl < w:
                r = 8 * ((lvl).bit_length())
                dst[r:P + T, sl] = src[r:P + T, sl] + src[r - lvl:P + T - lvl, sl]
                src, dst = dst, (pc_s if dst is pa_s else pa_s)
                lvl *= 2
            ssum = src[P:P + T, sl] + src[P - lvl:P + T - lvl, sl]
            tpos = (si * T + lax.broadcasted_iota(jnp.int32, (T, 1), 0) + 1).astype(F32)
            inv_cnt = 1.0 / jnp.minimum(tpos, float(w))
            pd_s[:, sl] = (ssum * inv_cnt - pb_s[P:P + T, sl]).astype(BF16)
            pb_s[0:P, sl] = pb_s[T:T + P, sl]

        @task(f"pm{g}", deps=(f"pool{g}",), m=64, v=16)
        def _():
            mixed = _dot(pd_s[:, sl], wpg_ref[g])
            po_s[:, sl] = (mixed * psc_ref[:, sl]).astype(BF16)

    for c in range(nchunk):
        make_conv(c)
        make_pool(c)
    all_pool = tuple(f"pm{g}" for g in range(nchunk))

    lam = lam_ref[...]
    ls8 = LRU_C * (jnp.minimum(lam, 0.0) - jnp.log1p(jnp.exp(-jnp.abs(lam))))
    rowi = lax.broadcasted_iota(jnp.int32, (SUBLANES, CW), 0)
    hcar = {}
    nstep = T // (2 * SUBLANES)

    def scan_rows(r0, sl, hprev):
        rows = slice(r0, r0 + SUBLANES)
        r = jax.nn.sigmoid(ga_s[rows, sl] + ba_ref[:, sl])
        i = jax.nn.sigmoid(gi_s[rows, sl] + bi_ref[:, sl])
        log_a = r * ls8[:, sl]
        a = jnp.exp(log_a)
        y = jnp.tanh(-log_a) * (a * a + 1.0)
        mult = jnp.where(y > 0.0, y * lax.rsqrt(y), 0.0)
        uu = mult * (i * xc_s[rows, sl])
        for kk in (1, 2, 4):
            keep = rowi >= kk
            a_sh = jnp.where(keep, pltpu.roll(a, kk, 0), 1.0)
            u_sh = jnp.where(keep, pltpu.roll(uu, kk, 0), 0.0)
            uu = a * u_sh + uu
            a = a * a_sh
        hrows = a * hprev + uu
        hlast = jnp.broadcast_to(hrows[SUBLANES - 1:SUBLANES, :], (SUBLANES, CW))
        return hrows * jax.nn.gelu(y_s[rows, sl]), hlast

    def make_scan(c, n):
        deps = (f"gm{c}", f"yr{c}") if n == 0 else (f"scan{c}_{n - 1}",)

        @task(f"scan{c}_{n}", deps=deps, v=70)
        def _():
            sl = cols(c)
            r0 = n * 2 * SUBLANES
            h = hst_s[:, sl] if n == 0 else hcar[c]
            if "epilogue" in anchor:
                h = _after_all(h, anchor.pop("epilogue"))
            if pending:
                h = _after(h, pending.pop()())
            out0, h = scan_rows(r0, sl, h)
            out1, h = scan_rows(r0 + SUBLANES, sl, h)
            ro_s[r0:r0 + 2 * SUBLANES, sl] = jnp.concatenate([out0, out1], axis=0).astype(BF16)
            if n == nstep - 1:
                hst_s[:, sl] = h
            else:
                hcar[c] = h

    for c in range(nchunk):
        for n in range(nstep):
            make_scan(c, n)
    all_scan = tuple(f"scan{c}_{nstep - 1}" for c in range(nchunk))

    def make_branch(name, dst, gate, gname, src, w_ref, deps, c):
        @task(f"{name}{c}", deps=deps + (f"{gname}{c}",), m=256, v=16)
        def _():
            dst[:, cols(c)] = gate[:, cols(c)] * _dot(src[...], w_ref[:, cols(c)])
            probe_of(dst, 0, c)

    def make_merge(c):
        @task(f"merge{c}", deps=all_scan + (f"g1_{c}", f"abr{c}", f"pbr{c}"), m=256, v=50)
        def _():
            sl = cols(c)
            rnn = g1_s[:, sl] * _dot(ro_s[...], wrnn_ref[:, sl])
            merged = (acc_s[:, sl] + rnn) + p2_s[:, sl]
            if c == nchunk - 1:
                merged = _after_all(merged, anchor.pop("prenorm_next"))
            mg_s[:, sl] = merged.astype(BF16)

    for c in range(nchunk):
        make_branch("abr", acc_s, g0_s, "g0_", ao_s, wattn_ref, all_att, c)
        make_branch("pbr", p2_s, g2_s, "g2_", po_s, wpool_ref, all_pool, c)
        make_merge(c)

    MXU_GAP = 640
    fillers = [f"xr{0}"]
    for c in range(1, nchunk):
        fillers += [f"q{c}", f"xr{c}", f"yr{c - 1}"]
    fillers += [f"yr{nchunk - 1}"] + [f"pp{c}" for c in range(nchunk)]
    fillers += [f"g{i}_{c}" for i in (0, 2, 1) for c in range(nchunk)]
    lead = 2
    main = [f"wo{0}", "kv", f"wo{1}", "q0"] + [f"wo{c}" for c in range(2, nchunk)] + fillers[:lead]
    rest = fillers[lead:]
    for n, (b, j) in enumerate(units):
        main.append(f"qk{b}_{j}")
        main += rest[n * len(rest) // len(units):(n + 1) * len(rest) // len(units)]
    main += [f"abr{c}" for c in range(nchunk)] + [f"pbr{c}" for c in range(nchunk)]
    n_before_merge = len(main)
    main += [f"merge{c}" for c in range(nchunk)]
    follow = {f"qk{b}_{j}": (f"sm{b}_{j}", f"pv{b}_{j}") for b, j in units}
    follow.update({f"xr{c}": (f"conv{c}", f"gm{c}") for c in range(nchunk)})
    follow.update({f"abr{c}": (f"pool{c}", f"pm{c}") for c in range(nchunk)})
    scan_next = [0] * nchunk
    done = []
    spent = [0]
    deferred = []

    def emit(name):
        if name in done:
            return
        fn, deps, m, v = tasks[name]
        for d in deps:
            emit(d)
        fn()
        done.append(name)
        spent[0] += m

    def scans_left():
        return sum(nstep - s for s in scan_next)

    def make_wout(c):
        @task(f"wo{c}", m=256)
        def _():
            mix_s[:, cols(c)] = _dot(mg_s[...], wout_ref[:, cols(c)])

    for c in range(nchunk):
        make_wout(c)

    @task("epilogue", deps=tuple(f"wo{c}" for c in range(nchunk)), v=250)
    def _():
        out = hprev_ref[...] + _rms(mix_s[...], gpost_ref[...])
        o_ref[...] = out
        anchor["epilogue"] = out

    @task("prenorm_next", v=200)
    def _():
        u_next = _rms(hnext_ref[...], gpre_ref[...])
        u_s[...] = u_next.astype(BF16)
        anchor["prenorm_next"] = u_next

    beside = {f"wo{nchunk - 1}": "epilogue", f"merge{0}": "prenorm_next"}
    for pos, name in enumerate(main):
        emit(name)
        if name in beside:
            emit(beside[name])
        for item in list(deferred):
            if spent[0] >= item[0]:
                emit(item[1])
                deferred.remove(item)
        if name in follow:
            producer, consumer = follow[name]
            emit(producer)
            deferred.append((spent[0] + MXU_GAP, consumer))
        quota = -(-scans_left() // max(1, n_before_merge - pos))
        for c in sorted(range(nchunk), key=lambda c: scan_next[c]):
            if quota > 0 and scan_next[c] < nstep and f"gm{c}" in done and f"yr{c}" in done:
                emit(f"scan{c}_{scan_next[c]}")
                scan_next[c] += 1
                quota -= 1
    assert set(done) == set(tasks), set(tasks) ^ set(done)

    k_s[:, 0:ATT_BLOCK, :] = k_s[:, T:T + ATT_BLOCK, :]
    v_s[:, 0:ATT_BLOCK, :] = v_s[:, T:T + ATT_BLOCK, :]


def _mlp_kernel(h_ref, gpre_ref, gpost_ref, wup_ref, wdown_ref, o_ref, m_s, acc_s):
    d_ff = wup_ref.shape[1]
    m_s[...] = _rms(h_ref[...], gpre_ref[...]).astype(BF16)
    for c in range(d_ff // FF_CHUNK):
        sl = slice(c * FF_CHUNK, (c + 1) * FF_CHUNK)
        up = jnp.maximum(_dot(m_s[...], wup_ref[:, sl]), 0.0)
        part = _dot((up * up).astype(BF16), wdown_ref[sl, :])
        if c == 0:
            acc_s[...] = part
        else:
            acc_s[...] = acc_s[...] + part
    o_ref[...] = h_ref[...] + _rms(acc_s[...], gpost_ref[...])


def _const_spec(shape):
    nd = len(shape)
    return pl.BlockSpec(shape, lambda *_: (0,) * nd, pipeline_mode=pl.Buffered(1))


def _mixer_call(h, tab, gpre, gpost, win, sinks, wattn, convw, convb, wg, ba, bi, lam,
                wrnn, wpg, psc, wpool, wout):
    B, S, D = h.shape
    T = MIX_TILE
    ns = S // T
    ntile = B * ns

    def tile_at(shift):
        def index_map(g):
            t = jnp.clip(g + shift, 0, ntile - 1)
            return (t // ns, t % ns, 0)
        return pl.BlockSpec((None, T, D), index_map)

    in_specs = [
        tile_at(0), tile_at(-1), tile_at(1),
        pl.BlockSpec((3, T, LANES), lambda g: (0, jnp.minimum(g, ntile - 1) % ns, 0)),
        _const_spec(gpre.shape), _const_spec(gpost.shape), _const_spec(win.shape),
        pl.BlockSpec(memory_space=pltpu.SMEM),
        _const_spec(wattn.shape), _const_spec(convw.shape), _const_spec(convb.shape),
        _const_spec(wg.shape), _const_spec(ba.shape), _const_spec(bi.shape),
        _const_spec(lam.shape), _const_spec(wrnn.shape), _const_spec(wpg.shape),
        _const_spec(psc.shape), _const_spec(wpool.shape), _const_spec(wout.shape),
    ]
    nvar = 2 * N_KV_HEADS
    scratch = [
        pltpu.VMEM((T, D), BF16),
        pltpu.VMEM((T, D), BF16),
        pltpu.VMEM((nvar, ATT_BLOCK + T, LANES), BF16),
        pltpu.VMEM((nvar, ATT_BLOCK + T, LANES), BF16),
        pltpu.VMEM((T, D), BF16),
        pltpu.VMEM((CONV_CARRY + T, D), F32),
        pltpu.VMEM((T, D), F32),
        pltpu.VMEM((T, D), F32),
        pltpu.VMEM((T, D), F32),
        pltpu.VMEM((T, D), F32),
        pltpu.VMEM((SUBLANES, D), F32),
        pltpu.VMEM((T, D), BF16),
        pltpu.VMEM((POOL_CARRY + T, D), F32),
        pltpu.VMEM((POOL_CARRY + T, D), F32),
        pltpu.VMEM((POOL_CARRY + T, D), F32),
        pltpu.VMEM((T, D), BF16),
        pltpu.VMEM((T, D), F32),
        pltpu.VMEM((T, D), F32),
        pltpu.VMEM((T, D), F32),
        pltpu.VMEM((T, D), F32),
        pltpu.VMEM((T, D), F32),
        pltpu.VMEM((T, D), BF16),
        pltpu.VMEM((2, 2, 2 * ATT_BLOCK, ATT_BLOCK), F32),
        pltpu.VMEM((2, 2, 2 * ATT_BLOCK, 2 * ATT_BLOCK), BF16),
        pltpu.VMEM((T, D), BF16),
        pltpu.VMEM((T, D), F32),
    ]
    return pl.pallas_call(
        functools.partial(_mixer_kernel, ns),
        grid=(ntile + 1,),
        in_specs=in_specs,
        out_specs=tile_at(-1),
        out_shape=jax.ShapeDtypeStruct(h.shape, h.dtype),
        scratch_shapes=scratch,
        compiler_params=pltpu.CompilerParams(
            dimension_semantics=("arbitrary",),
            vmem_limit_bytes=VMEM_LIMIT_BYTES),
        name="mixer",
    )(h, h, h, tab, gpre, gpost, win, sinks, wattn, convw, convb, wg, ba, bi, lam, wrnn,
      wpg, psc, wpool, wout)


def _mlp_call(h, gpre, gpost, wup, wdown):
    B, S, D = h.shape
    T = MLP_TILE
    tile = pl.BlockSpec((None, T, D), lambda b, s: (b, s, 0))
    return pl.pallas_call(
        _mlp_kernel,
        grid=(B, S // T),
        in_specs=[tile, _const_spec(gpre.shape), _const_spec(gpost.shape),
                  _const_spec(wup.shape), _const_spec(wdown.shape)],
        out_specs=tile,
        out_shape=jax.ShapeDtypeStruct(h.shape, h.dtype),
        scratch_shapes=[pltpu.VMEM((T, D), BF16), pltpu.VMEM((T, D), F32)],
        compiler_params=pltpu.CompilerParams(
            dimension_semantics=("arbitrary", "arbitrary"),
            vmem_limit_bytes=VMEM_LIMIT_BYTES),
        name="mlp",
    )(h, gpre, gpost, wup, wdown)


def _rope_tables(seq):
    half = ROT_DIM // 2
    inv_freq = ROPE_THETA ** (-jnp.arange(0, ROT_DIM, 2, dtype=F32) / ROT_DIM)
    ang = jnp.arange(seq, dtype=F32)[:, None] * inv_freq[None, :]
    cos, sin = jnp.cos(ang), jnp.sin(ang)
    pad = jnp.zeros((seq, HEAD_DIM - ROT_DIM), F32)
    c = jnp.concatenate([cos, cos, pad + 1.0], axis=1)
    s1 = jnp.concatenate([-sin, jnp.zeros_like(sin), pad], axis=1)
    s2 = jnp.concatenate([jnp.zeros_like(sin), sin, pad], axis=1)
    tab = jnp.stack([c, s1, s2])
    tab = jnp.concatenate([tab] * (LANES // HEAD_DIM), axis=2)
    return tab * (float(HEAD_DIM) ** -0.5)


def _gate_weights(w_a, w_i):
    L, nb, bw, _ = w_a.shape

    def bd(w):
        w = w.reshape(L, nb // 2, 2, bw, bw)
        z = jnp.zeros_like(w[:, :, 0])
        top = jnp.concatenate([w[:, :, 0], z], axis=-1)
        bot = jnp.concatenate([z, w[:, :, 1]], axis=-1)
        return jnp.concatenate([top, bot], axis=-2)

    return jnp.concatenate([bd(w_a), bd(w_i)], axis=-1).astype(BF16)


def kernel(x, norm_mix_pre, norm_mix_post, w_in, attn_sinks, w_attn_br, conv_w, conv_b,
           w_rg_a, b_rg_a, w_rg_i, b_rg_i, lru_lambda, w_rnn_br, w_pool_groups, pool_scale,
           w_pool_br, w_out, norm_mlp_pre, norm_mlp_post, w_mlp_up, w_mlp_down):
    B, S, D = x.shape
    depth = w_in.shape[0]
    assert S % MIX_TILE == 0 and S % MLP_TILE == 0 and MIX_TILE % ATT_BLOCK == 0
    tab = _rope_tables(S)
    wg = _gate_weights(w_rg_a, w_rg_i)
    row = lambda a, l: a[l][None, :]
    h = x
    for l in range(depth):
        h = _mixer_call(
            h, tab, row(norm_mix_pre, l), row(norm_mix_post, l), w_in[l].astype(BF16),
            attn_sinks[l], w_attn_br[l].astype(BF16), conv_w[l], row(conv_b, l), wg[l],
            row(b_rg_a, l), row(b_rg_i, l), row(lru_lambda, l), w_rnn_br[l].astype(BF16),
            w_pool_groups[l].astype(BF16), row(pool_scale, l), w_pool_br[l].astype(BF16),
            w_out[l].astype(BF16))
        h = _mlp_call(h, row(norm_mlp_pre, l), row(norm_mlp_post, l),
                      w_mlp_up[l].astype(BF16), w_mlp_down[l].astype(BF16))
    return h
```

```python
import functools

import jax
import jax.numpy as jnp
from jax import lax
from jax.experimental import pallas as pl
from jax.experimental.pallas import tpu as pltpu

F32 = jnp.float32
BF16 = jnp.bfloat16

HEAD_DIM = 64
N_Q_HEADS = 16
N_KV_HEADS = 4
ATT_BLOCK = 128
ROT_DIM = HEAD_DIM // 4
ROPE_THETA = 500000.0
RNN_BW = 64
CONV_WIDTH = 4
LRU_C = 8.0
POOL_WINDOWS = (2, 4, 8, 16)
EPS = 1e-6
NEG_INF = -1e30

LANES = 128
SUBLANES = 8
CONV_CARRY = SUBLANES
POOL_CARRY = 32
VMEM_LIMIT_BYTES = 58 * 1024 * 1024

MIX_TILE = 256
MLP_TILE = 1024
FF_CHUNK = 1024


def _dot(a, b):
    return jnp.dot(a, b, preferred_element_type=F32)


def _dot_nt(a, b):
    return lax.dot_general(a, b, (((1,), (1,)), ((), ())), preferred_element_type=F32)


def _rms(x, g):
    ms = jnp.mean(x * x, axis=-1, keepdims=True)
    return (x * lax.rsqrt(ms + EPS)) * g


def _rope(x, c, s1, s2):
    half = ROT_DIM // 2
    return x * c + pltpu.roll(x, LANES - half, 1) * s1 + pltpu.roll(x, half, 1) * s2


def _after(x, probe):
    sh = jnp.uint32(16)
    z = lax.shift_right_logical(lax.shift_right_logical(pltpu.bitcast(probe, jnp.uint32), sh), sh)
    return pltpu.bitcast(pltpu.bitcast(x, jnp.uint32) | z, F32)


def _after_all(x, y):
    acc = y[0:SUBLANES, :]
    for r in range(SUBLANES, y.shape[0], SUBLANES):
        acc = acc + y[r:r + SUBLANES, :]
    blk = acc[:, 0:LANES]
    for c in range(LANES, y.shape[1], LANES):
        blk = blk + acc[:, c:c + LANES]
    top = _after(x[0:SUBLANES, 0:LANES], blk)
    if x.shape[1] > LANES:
        top = jnp.concatenate([top, x[0:SUBLANES, LANES:]], axis=1)
    if x.shape[0] > SUBLANES:
        top = jnp.concatenate([top, x[SUBLANES:, :]], axis=0)
    return top


def _head_variants(x, kc):
    lane = lax.broadcasted_iota(jnp.int32, x.shape, 1)
    lo = lane < HEAD_DIM
    xs = pltpu.roll(x, HEAD_DIM, 1)
    zero = jnp.zeros_like(x)
    return {
        (2 * kc, 0): jnp.where(lo, x, zero),
        (2 * kc, 1): jnp.where(lo, zero, xs),
        (2 * kc + 1, 0): jnp.where(lo, xs, zero),
        (2 * kc + 1, 1): jnp.where(lo, zero, x),
    }


def _mixer_kernel(seq_tiles, h_ref, hprev_ref, hnext_ref, tab_ref, gpre_ref, gpost_ref,
                  win_ref, sink_ref, wattn_ref,
                  convw_ref, convb_ref, wg_ref, ba_ref, bi_ref, lam_ref, wrnn_ref,
                  wpg_ref, psc_ref, wpool_ref, wout_ref, o_ref,
                  u_s, q_s, k_s, v_s, ao_s, xb_s, xc_s, ga_s, gi_s, y_s, hst_s, ro_s,
                  pb_s, pa_s, pc_s, po_s, acc_s, g0_s, g1_s, g2_s, p2_s, mg_s,
                  sc_s, pw_s, pd_s, mix_s):
    g = pl.program_id(0)
    si = lax.rem(g, seq_tiles)
    T, D = h_ref.shape
    nblk = T // ATT_BLOCK

    @pl.when(g == 0)
    def _():
        u_s[...] = _rms(h_ref[...], gpre_ref[...]).astype(BF16)
        mg_s[...] = jnp.zeros(mg_s.shape, BF16)

    @pl.when(si == 0)
    def _():
        k_s[:, 0:ATT_BLOCK, :] = jnp.zeros((2 * N_KV_HEADS, ATT_BLOCK, LANES), BF16)
        v_s[:, 0:ATT_BLOCK, :] = jnp.zeros((2 * N_KV_HEADS, ATT_BLOCK, LANES), BF16)
        xb_s[0:CONV_CARRY, :] = jnp.zeros((CONV_CARRY, D), F32)
        pb_s[0:POOL_CARRY, :] = jnp.zeros((POOL_CARRY, D), F32)
        hst_s[...] = jnp.zeros(hst_s.shape, F32)

    kvw = N_KV_HEADS * HEAD_DIM
    base = D + 2 * kvw
    gate0 = base + 3 * D
    P = POOL_CARRY
    CW = D // len(POOL_WINDOWS)
    nchunk = D // CW
    tasks = {}
    anchor = {}

    def task(name, deps=(), m=0, v=0):
        def reg(fn):
            tasks[name] = (fn, tuple(deps), m, v)
            return fn
        return reg

    def cols(c):
        return slice(c * CW, (c + 1) * CW)

    def proj(col0, width=CW):
        return _dot(u_s[...], win_ref[:, col0:col0 + width])

    c_q, s1_q, s2_q = tab_ref[0], tab_ref[1], tab_ref[2]
    inv_scale = float(HEAD_DIM) ** 0.5
    cur = slice(ATT_BLOCK, ATT_BLOCK + T)

    @task("kv", m=128, v=150)
    def _():
        c_k, s1_k, s2_k = c_q * inv_scale, s1_q * inv_scale, s2_q * inv_scale
        k = proj(D, kvw)
        v = proj(D + kvw, kvw)
        for kc in range(kvw // LANES):
            sl = slice(kc * LANES, (kc + 1) * LANES)
            for (head, var), val in _head_variants(_rope(k[:, sl], c_k, s1_k, s2_k), kc).items():
                k_s[2 * head + var, cur, :] = val.astype(BF16)
            for (head, var), val in _head_variants(v[:, sl], kc).items():
                v_s[2 * head + var, cur, :] = val.astype(BF16)

    def make_q(j):
        @task(f"q{j}", m=256, v=100)
        def _():
            q = proj(2 * j * LANES, 2 * LANES)
            for c in range(2):
                sl = slice((2 * j + c) * LANES, (2 * j + c + 1) * LANES)
                q_s[:, sl] = _rope(q[:, c * LANES:(c + 1) * LANES], c_q, s1_q, s2_q).astype(BF16)

    row = lax.broadcasted_iota(jnp.int32, (2 * ATT_BLOCK, ATT_BLOCK), 0)
    col = lax.broadcasted_iota(jnp.int32, (2 * ATT_BLOCK, ATT_BLOCK), 1)
    qpos = jnp.where(row >= ATT_BLOCK, row - ATT_BLOCK, row)
    from_prev = col > qpos
    top = row < ATT_BLOCK

    def make_attn(b, j):
        r0 = b * ATT_BLOCK
        slot = (j * nblk + b) % 2

        @task(f"qk{b}_{j}", deps=("kv", f"q{j}"), m=128, v=40)
        def _():
            qst = jnp.concatenate(
                [q_s[r0:r0 + ATT_BLOCK, (2 * j) * LANES:(2 * j + 1) * LANES],
                 q_s[r0:r0 + ATT_BLOCK, (2 * j + 1) * LANES:(2 * j + 2) * LANES]], axis=0)
            for var in range(2):
                kcat = k_s[2 * j + var, r0:r0 + 2 * ATT_BLOCK, :]
                s = _dot_nt(qst, kcat)
                s_prev = s[:, 0:ATT_BLOCK]
                if b == 0:
                    s_prev = jnp.where(si > 0, s_prev, NEG_INF)
                sc_s[slot, var] = jnp.where(from_prev, s_prev, s[:, ATT_BLOCK:])

        @task(f"sm{b}_{j}", deps=(f"qk{b}_{j}",), v=220)
        def _():
            for var in range(2):
                sc = sc_s[slot, var]
                sink = jnp.where(top, sink_ref[4 * j + var], sink_ref[4 * j + 2 + var])[:, 0:1]
                m = jnp.maximum(jnp.max(sc, axis=-1, keepdims=True), sink)
                p = jnp.exp(sc - m)
                denom = jnp.sum(p, axis=-1, keepdims=True) + jnp.exp(sink - m)
                p = p * (1.0 / denom)
                zero = jnp.zeros_like(p)
                pw_s[slot, var] = jnp.concatenate(
                    [jnp.where(from_prev, p, zero), jnp.where(from_prev, zero, p)],
                    axis=1).astype(BF16)

        @task(f"pv{b}_{j}", deps=(f"sm{b}_{j}",), m=128, v=20)
        def _():
            o = None
            for var in range(2):
                vcat = v_s[2 * j + var, r0:r0 + 2 * ATT_BLOCK, :]
                ov = _dot(pw_s[slot, var], vcat)
                o = ov if o is None else o + ov
            ao_s[r0:r0 + ATT_BLOCK, (2 * j) * LANES:(2 * j + 1) * LANES] = (
                o[0:ATT_BLOCK].astype(BF16))
            ao_s[r0:r0 + ATT_BLOCK, (2 * j + 1) * LANES:(2 * j + 2) * LANES] = (
                o[ATT_BLOCK:].astype(BF16))

    units = [(b, j) for j in range(N_KV_HEADS) for b in range(nblk)]
    for j in range(N_KV_HEADS):
        make_q(j)
    for b, j in units:
        make_attn(b, j)
    all_att = tuple(f"pv{b}_{j}" for b, j in units)

    def make_proj(name, dst, row0, col0, c):
        @task(f"{name}{c}", m=256)
        def _():
            dst[row0:row0 + T, cols(c)] = proj(col0 + c * CW)

    def make_gate(name, dst, idx, c):
        @task(f"{name}{c}", m=256, v=60)
        def _():
            dst[:, cols(c)] = jax.nn.sigmoid(proj(gate0 + idx * D + c * CW))

    for c in range(nchunk):
        make_proj("xr", xb_s, CONV_CARRY, base, c)
        make_proj("yr", y_s, 0, base + D, c)
        make_proj("pp", pb_s, P, base + 2 * D, c)
        make_gate("g0_", g0_s, 0, c)
        make_gate("g1_", g1_s, 1, c)
        make_gate("g2_", g2_s, 2, c)

    def make_conv(c):
        @task(f"conv{c}", deps=(f"xr{c}",), v=330)
        def _():
            sl = cols(c)
            xall = xb_s[:, sl]
            xc = convb_ref[:, sl]
            for tap in range(CONV_WIDTH):
                lag = CONV_WIDTH - 1 - tap
                xlag = pltpu.roll(xall, lag, 0) if lag else xall
                xc = xc + xlag[CONV_CARRY:, :] * convw_ref[tap:tap + 1, sl]
            xc_s[:, sl] = xc
            xb_s[0:CONV_CARRY, sl] = xb_s[T:T + CONV_CARRY, sl]

        @task(f"gm{c}", deps=(f"conv{c}",), m=64, v=16)
        def _():
            for lc in range(c * CW // LANES, (c + 1) * CW // LANES):
                ls = slice(lc * LANES, (lc + 1) * LANES)
                res = _dot(xc_s[:, ls].astype(BF16), wg_ref[lc])
                ga_s[:, ls] = res[:, 0:LANES]
                gi_s[:, ls] = res[:, LANES:]

    def make_pool(g):
        sl = cols(g)
        w = POOL_WINDOWS[g]

        @task(f"pool{g}", deps=(f"pp{g}",), v=180)
        def _():
            src, dst, lvl = pb_s, pa_s, 1
            while 2 * lvl < w:
                r = 8 * ((lvl).bit_length())
                dst[r:P + T, sl] = src[r:P + T, sl] + src[r - lvl:P + T - lvl, sl]
                src, dst = dst, (pc_s if dst is pa_s else pa_s)
                lvl *= 2
            ssum = src[P:P + T, sl] + src[P - lvl:P + T - lvl, sl]
            tpos = (si * T + lax.broadcasted_iota(jnp.int32, (T, 1), 0) + 1).astype(F32)
            inv_cnt = 1.0 / jnp.minimum(tpos, float(w))
            pd_s[:, sl] = (ssum * inv_cnt - pb_s[P:P + T, sl]).astype(BF16)
            pb_s[0:P, sl] = pb_s[T:T + P, sl]

        @task(f"pm{g}", deps=(f"pool{g}",), m=64, v=16)
        def _():
            mixed = _dot(pd_s[:, sl], wpg_ref[g])
            po_s[:, sl] = (mixed * psc_ref[:, sl]).astype(BF16)

    for c in range(nchunk):
        make_conv(c)
        make_pool(c)
    all_pool = tuple(f"pm{g}" for g in range(nchunk))

    lam = lam_ref[...]
    ls8 = LRU_C * (jnp.minimum(lam, 0.0) - jnp.log1p(jnp.exp(-jnp.abs(lam))))
    rowi = lax.broadcasted_iota(jnp.int32, (SUBLANES, CW), 0)
    hcar = {}
    nstep = T // (2 * SUBLANES)

    def scan_rows(r0, sl, hprev):
        rows = slice(r0, r0 + SUBLANES)
        r = jax.nn.sigmoid(ga_s[rows, sl] + ba_ref[:, sl])
        i = jax.nn.sigmoid(gi_s[rows, sl] + bi_ref[:, sl])
        log_a = r * ls8[:, sl]
        a = jnp.exp(log_a)
        y = jnp.tanh(-log_a) * (a * a + 1.0)
        mult = jnp.where(y > 0.0, y * lax.rsqrt(y), 0.0)
        uu = mult * (i * xc_s[rows, sl])
        for kk in (1, 2, 4):
            keep = rowi >= kk
            a_sh = jnp.where(keep, pltpu.roll(a, kk, 0), 1.0)
            u_sh = jnp.where(keep, pltpu.roll(uu, kk, 0), 0.0)
            uu = a * u_sh + uu
            a = a * a_sh
        hrows = a * hprev + uu
        hlast = jnp.broadcast_to(hrows[SUBLANES - 1:SUBLANES, :], (SUBLANES, CW))
        return hrows * jax.nn.gelu(y_s[rows, sl]), hlast

    def make_scan(c, n):
        deps = (f"gm{c}", f"yr{c}") if n == 0 else (f"scan{c}_{n - 1}",)

        @task(f"scan{c}_{n}", deps=deps, v=70)
        def _():
            sl = cols(c)
            r0 = n * 2 * SUBLANES
            h = hst_s[:, sl] if n == 0 else hcar[c]
            if "epilogue" in anchor:
                h = _after_all(h, anchor.pop("epilogue"))
            out0, h = scan_rows(r0, sl, h)
            out1, h = scan_rows(r0 + SUBLANES, sl, h)
            ro_s[r0:r0 + 2 * SUBLANES, sl] = jnp.concatenate([out0, out1], axis=0).astype(BF16)
            if n == nstep - 1:
                hst_s[:, sl] = h
            else:
                hcar[c] = h

    for c in range(nchunk):
        for n in range(nstep):
            make_scan(c, n)
    all_scan = tuple(f"scan{c}_{nstep - 1}" for c in range(nchunk))

    def make_branch(name, dst, gate, gname, src, w_ref, deps, c):
        @task(f"{name}{c}", deps=deps + (f"{gname}{c}",), m=256, v=16)
        def _():
            dst[:, cols(c)] = gate[:, cols(c)] * _dot(src[...], w_ref[:, cols(c)])

    def make_merge(c):
        @task(f"merge{c}", deps=all_scan + (f"g1_{c}", f"abr{c}", f"pbr{c}"), m=256, v=50)
        def _():
            sl = cols(c)
            rnn = g1_s[:, sl] * _dot(ro_s[...], wrnn_ref[:, sl])
            merged = (acc_s[:, sl] + rnn) + p2_s[:, sl]
            if c == nchunk - 1:
                merged = _after_all(merged, anchor.pop("prenorm_next"))
            mg_s[:, sl] = merged.astype(BF16)

    for c in range(nchunk):
        make_branch("abr", acc_s, g0_s, "g0_", ao_s, wattn_ref, all_att, c)
        make_branch("pbr", p2_s, g2_s, "g2_", po_s, wpool_ref, all_pool, c)
        make_merge(c)

    MXU_GAP = 640
    fillers = [f"xr{0}"]
    for c in range(1, nchunk):
        fillers += [f"q{c}", f"xr{c}", f"yr{c - 1}"]
    fillers += [f"yr{nchunk - 1}"] + [f"pp{c}" for c in range(nchunk)]
    fillers += [f"g{i}_{c}" for i in (0, 2, 1) for c in range(nchunk)]
    lead = 2
    main = [f"wo{0}", "kv", f"wo{1}", "q0"] + [f"wo{c}" for c in range(2, nchunk)] + fillers[:lead]
    rest = fillers[lead:]
    for n, (b, j) in enumerate(units):
        main.append(f"qk{b}_{j}")
        main += rest[n * len(rest) // len(units):(n + 1) * len(rest) // len(units)]
    main += [f"abr{c}" for c in range(nchunk)] + [f"pbr{c}" for c in range(nchunk)]
    n_before_merge = len(main)
    main += [f"merge{c}" for c in range(nchunk)]
    follow = {f"qk{b}_{j}": (f"sm{b}_{j}", f"pv{b}_{j}") for b, j in units}
    follow.update({f"xr{c}": (f"conv{c}", f"gm{c}") for c in range(nchunk)})
    follow.update({f"abr{c}": (f"pool{c}", f"pm{c}") for c in range(nchunk)})
    scan_next = [0] * nchunk
    done = []
    spent = [0]
    deferred = []

    def emit(name):
        if name in done:
            return
        fn, deps, m, v = tasks[name]
        for d in deps:
            emit(d)
        fn()
        done.append(name)
        spent[0] += m

    def scans_left():
        return sum(nstep - s for s in scan_next)

    def make_wout(c):
        @task(f"wo{c}", m=256)
        def _():
            mix_s[:, cols(c)] = _dot(mg_s[...], wout_ref[:, cols(c)])

    for c in range(nchunk):
        make_wout(c)

    @task("epilogue", deps=tuple(f"wo{c}" for c in range(nchunk)), v=250)
    def _():
        out = hprev_ref[...] + _rms(mix_s[...], gpost_ref[...])
        o_ref[...] = out
        anchor["epilogue"] = out

    @task("prenorm_next", v=200)
    def _():
        u_next = _rms(hnext_ref[...], gpre_ref[...])
        u_s[...] = u_next.astype(BF16)
        anchor["prenorm_next"] = u_next

    beside = {f"wo{nchunk - 1}": "epilogue", f"merge{0}": "prenorm_next"}
    for pos, name in enumerate(main):
        emit(name)
        if name in beside:
            emit(beside[name])
        for item in list(deferred):
            if spent[0] >= item[0]:
                emit(item[1])
                deferred.remove(item)
        if name in follow:
            producer, consumer = follow[name]
            emit(producer)
            deferred.append((spent[0] + MXU_GAP, consumer))
        quota = -(-scans_left() // max(1, n_before_merge - pos))
        for c in sorted(range(nchunk), key=lambda c: scan_next[c]):
            if quota > 0 and scan_next[c] < nstep and f"gm{c}" in done and f"yr{c}" in done:
                emit(f"scan{c}_{scan_next[c]}")
                scan_next[c] += 1
                quota -= 1
    assert set(done) == set(tasks), set(tasks) ^ set(done)

    k_s[:, 0:ATT_BLOCK, :] = k_s[:, T:T + ATT_BLOCK, :]
    v_s[:, 0:ATT_BLOCK, :] = v_s[:, T:T + ATT_BLOCK, :]


def _mlp_kernel(h_ref, gpre_ref, gpost_ref, wup_ref, wdown_ref, o_ref, m_s, acc_s):
    d_ff = wup_ref.shape[1]
    m_s[...] = _rms(h_ref[...], gpre_ref[...]).astype(BF16)
    for c in range(d_ff // FF_CHUNK):
        sl = slice(c * FF_CHUNK, (c + 1) * FF_CHUNK)
        up = jnp.maximum(_dot(m_s[...], wup_ref[:, sl]), 0.0)
        part = _dot((up * up).astype(BF16), wdown_ref[sl, :])
        if c == 0:
            acc_s[...] = part
        else:
            acc_s[...] = acc_s[...] + part
    o_ref[...] = h_ref[...] + _rms(acc_s[...], gpost_ref[...])


def _const_spec(shape):
    nd = len(shape)
    return pl.BlockSpec(shape, lambda *_: (0,) * nd, pipeline_mode=pl.Buffered(1))


def _mixer_call(h, tab, gpre, gpost, win, sinks, wattn, convw, convb, wg, ba, bi, lam,
                wrnn, wpg, psc, wpool, wout):
    B, S, D = h.shape
    T = MIX_TILE
    ns = S // T
    ntile = B * ns

    def tile_at(shift):
        def index_map(g):
            t = jnp.clip(g + shift, 0, ntile - 1)
            return (t // ns, t % ns, 0)
        return pl.BlockSpec((None, T, D), index_map)

    in_specs = [
        tile_at(0), tile_at(-1), tile_at(1),
        pl.BlockSpec((3, T, LANES), lambda g: (0, jnp.minimum(g, ntile - 1) % ns, 0)),
        _const_spec(gpre.shape), _const_spec(gpost.shape), _const_spec(win.shape),
        pl.BlockSpec(memory_space=pltpu.SMEM),
        _const_spec(wattn.shape), _const_spec(convw.shape), _const_spec(convb.shape),
        _const_spec(wg.shape), _const_spec(ba.shape), _const_spec(bi.shape),
        _const_spec(lam.shape), _const_spec(wrnn.shape), _const_spec(wpg.shape),
        _const_spec(psc.shape), _const_spec(wpool.shape), _const_spec(wout.shape),
    ]
    nvar = 2 * N_KV_HEADS
    scratch = [
        pltpu.VMEM((T, D), BF16),
        pltpu.VMEM((T, D), BF16),
        pltpu.VMEM((nvar, ATT_BLOCK + T, LANES), BF16),
        pltpu.VMEM((nvar, ATT_BLOCK + T, LANES), BF16),
        pltpu.VMEM((T, D), BF16),
        pltpu.VMEM((CONV_CARRY + T, D), F32),
        pltpu.VMEM((T, D), F32),
        pltpu.VMEM((T, D), F32),
        pltpu.VMEM((T, D), F32),
        pltpu.VMEM((T, D), F32),
        pltpu.VMEM((SUBLANES, D), F32),
        pltpu.VMEM((T, D), BF16),
        pltpu.VMEM((POOL_CARRY + T, D), F32),
        pltpu.VMEM((POOL_CARRY + T, D), F32),
        pltpu.VMEM((POOL_CARRY + T, D), F32),
        pltpu.VMEM((T, D), BF16),
        pltpu.VMEM((T, D), F32),
        pltpu.VMEM((T, D), F32),
        pltpu.VMEM((T, D), F32),
        pltpu.VMEM((T, D), F32),
        pltpu.VMEM((T, D), F32),
        pltpu.VMEM((T, D), BF16),
        pltpu.VMEM((2, 2, 2 * ATT_BLOCK, ATT_BLOCK), F32),
        pltpu.VMEM((2, 2, 2 * ATT_BLOCK, 2 * ATT_BLOCK), BF16),
        pltpu.VMEM((T, D), BF16),
        pltpu.VMEM((T, D), F32),
    ]
    return pl.pallas_call(
        functools.partial(_mixer_kernel, ns),
        grid=(ntile + 1,),
        in_specs=in_specs,
        out_specs=tile_at(-1),
        out_shape=jax.ShapeDtypeStruct(h.shape, h.dtype),
        scratch_shapes=scratch,
        compiler_params=pltpu.CompilerParams(
            dimension_semantics=("arbitrary",),
            vmem_limit_bytes=VMEM_LIMIT_BYTES),
        name="mixer",
    )(h, h, h, tab, gpre, gpost, win, sinks, wattn, convw, convb, wg, ba, bi, lam, wrnn,
      wpg, psc, wpool, wout)


def _mlp_call(h, gpre, gpost, wup, wdown):
    B, S, D = h.shape
    T = MLP_TILE
    tile = pl.BlockSpec((None, T, D), lambda b, s: (b, s, 0))
    return pl.pallas_call(
        _mlp_kernel,
        grid=(B, S // T),
        in_specs=[tile, _const_spec(gpre.shape), _const_spec(gpost.shape),
                  _const_spec(wup.shape), _const_spec(wdown.shape)],
        out_specs=tile,
        out_shape=jax.ShapeDtypeStruct(h.shape, h.dtype),
        scratch_shapes=[pltpu.VMEM((T, D), BF16), pltpu.VMEM((T, D), F32)],
        compiler_params=pltpu.CompilerParams(
            dimension_semantics=("arbitrary", "arbitrary"),
            vmem_limit_bytes=VMEM_LIMIT_BYTES),
        name="mlp",
    )(h, gpre, gpost, wup, wdown)


def _rope_tables(seq):
    half = ROT_DIM // 2
    inv_freq = ROPE_THETA ** (-jnp.arange(0, ROT_DIM, 2, dtype=F32) / ROT_DIM)
    ang = jnp.arange(seq, dtype=F32)[:, None] * inv_freq[None, :]
    cos, sin = jnp.cos(ang), jnp.sin(ang)
    pad = jnp.zeros((seq, HEAD_DIM - ROT_DIM), F32)
    c = jnp.concatenate([cos, cos, pad + 1.0], axis=1)
    s1 = jnp.concatenate([-sin, jnp.zeros_like(sin), pad], axis=1)
    s2 = jnp.concatenate([jnp.zeros_like(sin), sin, pad], axis=1)
    tab = jnp.stack([c, s1, s2])
    tab = jnp.concatenate([tab] * (LANES // HEAD_DIM), axis=2)
    return tab * (float(HEAD_DIM) ** -0.5)


def _gate_weights(w_a, w_i):
    L, nb, bw, _ = w_a.shape

    def bd(w):
        w = w.reshape(L, nb // 2, 2, bw, bw)
        z = jnp.zeros_like(w[:, :, 0])
        top = jnp.concatenate([w[:, :, 0], z], axis=-1)
        bot = jnp.concatenate([z, w[:, :, 1]], axis=-1)
        return jnp.concatenate([top, bot], axis=-2)

    return jnp.concatenate([bd(w_a), bd(w_i)], axis=-1).astype(BF16)


def kernel(x, norm_mix_pre, norm_mix_post, w_in, attn_sinks, w_attn_br, conv_w, conv_b,
           w_rg_a, b_rg_a, w_rg_i, b_rg_i, lru_lambda, w_rnn_br, w_pool_groups, pool_scale,
           w_pool_br, w_out, norm_mlp_pre, norm_mlp_post, w_mlp_up, w_mlp_down):
    B, S, D = x.shape
    depth = w_in.shape[0]
    assert S % MIX_TILE == 0 and S % MLP_TILE == 0 and MIX_TILE % ATT_BLOCK == 0
    tab = _rope_tables(S)
    wg = _gate_weights(w_rg_a, w_rg_i)
    row = lambda a, l: a[l][None, :]
    h = x
    for l in range(depth):
        h = _mixer_call(
            h, tab, row(norm_mix_pre, l), row(norm_mix_post, l), w_in[l].astype(BF16),
            attn_sinks[l], w_attn_br[l].astype(BF16), conv_w[l], row(conv_b, l), wg[l],
            row(b_rg_a, l), row(b_rg_i, l), row(lru_lambda, l), w_rnn_br[l].astype(BF16),
            w_pool_groups[l].astype(BF16), row(pool_scale, l), w_pool_br[l].astype(BF16),
            w_out[l].astype(BF16))
        h = _mlp_call(h, row(norm_mlp_pre, l), row(norm_mlp_post, l),
                      w_mlp_up[l].astype(BF16), w_mlp_down[l].astype(BF16))
    return h
```

```python
import functools

import jax
import jax.numpy as jnp
from jax import lax
from jax.experimental import pallas as pl
from jax.experimental.pallas import tpu as pltpu

F32 = jnp.float32
BF16 = jnp.bfloat16

HEAD_DIM = 64
N_Q_HEADS = 16
N_KV_HEADS = 4
ATT_BLOCK = 128
ROT_DIM = HEAD_DIM // 4
ROPE_THETA = 500000.0
RNN_BW = 64
CONV_WIDTH = 4
LRU_C = 8.0
POOL_WINDOWS = (2, 4, 8, 16)
EPS = 1e-6
NEG_INF = -1e30

LANES = 128
SUBLANES = 8
CONV_CARRY = SUBLANES
POOL_CARRY = 32
VMEM_LIMIT_BYTES = 58 * 1024 * 1024

MIX_TILE = 256
MLP_TILE = 1024
FF_CHUNK = 1024


def _dot(a, b):
    return jnp.dot(a, b, preferred_element_type=F32)


def _dot_nt(a, b):
    return lax.dot_general(a, b, (((1,), (1,)), ((), ())), preferred_element_type=F32)


def _rms(x, g):
    ms = jnp.mean(x * x, axis=-1, keepdims=True)
    return (x * lax.rsqrt(ms + EPS)) * g


def _rope(x, c, s1, s2):
    half = ROT_DIM // 2
    return x * c + pltpu.roll(x, LANES - half, 1) * s1 + pltpu.roll(x, half, 1) * s2


def _after(x, probe):
    sh = jnp.uint32(16)
    z = lax.shift_right_logical(lax.shift_right_logical(pltpu.bitcast(probe, jnp.uint32), sh), sh)
    return pltpu.bitcast(pltpu.bitcast(x, jnp.uint32) | z, F32)


def _after_all(x, y):
    acc = y[0:SUBLANES, :]
    for r in range(SUBLANES, y.shape[0], SUBLANES):
        acc = acc + y[r:r + SUBLANES, :]
    blk = acc[:, 0:LANES]
    for c in range(LANES, y.shape[1], LANES):
        blk = blk + acc[:, c:c + LANES]
    top = _after(x[0:SUBLANES, 0:LANES], blk)
    if x.shape[1] > LANES:
        top = jnp.concatenate([top, x[0:SUBLANES, LANES:]], axis=1)
    if x.shape[0] > SUBLANES:
        top = jnp.concatenate([top, x[SUBLANES:, :]], axis=0)
    return top


def _head_variants(x, kc):
    lane = lax.broadcasted_iota(jnp.int32, x.shape, 1)
    lo = lane < HEAD_DIM
    xs = pltpu.roll(x, HEAD_DIM, 1)
    zero = jnp.zeros_like(x)
    return {
        (2 * kc, 0): jnp.where(lo, x, zero),
        (2 * kc, 1): jnp.where(lo, zero, xs),
        (2 * kc + 1, 0): jnp.where(lo, xs, zero),
        (2 * kc + 1, 1): jnp.where(lo, zero, x),
    }


def _mixer_kernel(seq_tiles, h_ref, hprev_ref, hnext_ref, tab_ref, gpre_ref, gpost_ref,
                  win_ref, sink_ref, wattn_ref,
                  convw_ref, convb_ref, wg_ref, ba_ref, bi_ref, lam_ref, wrnn_ref,
                  wpg_ref, psc_ref, wpool_ref, wout_ref, o_ref,
                  u_s, q_s, k_s, v_s, ao_s, xb_s, xc_s, ga_s, gi_s, y_s, hst_s, ro_s,
                  pb_s, pa_s, pc_s, po_s, acc_s, g0_s, g1_s, g2_s, p2_s, mg_s,
                  sc_s, pw_s, pd_s, mix_s):
    g = pl.program_id(0)
    si = lax.rem(g, seq_tiles)
    T, D = h_ref.shape
    nblk = T // ATT_BLOCK

    @pl.when(g == 0)
    def _():
        u_s[...] = _rms(h_ref[...], gpre_ref[...]).astype(BF16)
        mg_s[...] = jnp.zeros(mg_s.shape, BF16)

    @pl.when(si == 0)
    def _():
        k_s[:, 0:ATT_BLOCK, :] = jnp.zeros((2 * N_KV_HEADS, ATT_BLOCK, LANES), BF16)
        v_s[:, 0:ATT_BLOCK, :] = jnp.zeros((2 * N_KV_HEADS, ATT_BLOCK, LANES), BF16)
        xb_s[0:CONV_CARRY, :] = jnp.zeros((CONV_CARRY, D), F32)
        pb_s[0:POOL_CARRY, :] = jnp.zeros((POOL_CARRY, D), F32)
        hst_s[...] = jnp.zeros(hst_s.shape, F32)

    kvw = N_KV_HEADS * HEAD_DIM
    base = D + 2 * kvw
    gate0 = base + 3 * D
    P = POOL_CARRY
    CW = D // len(POOL_WINDOWS)
    nchunk = D // CW
    tasks = {}
    anchor = {}

    def task(name, deps=(), m=0, v=0):
        def reg(fn):
            tasks[name] = (fn, tuple(deps), m, v)
            return fn
        return reg

    def cols(c):
        return slice(c * CW, (c + 1) * CW)

    def proj(col0, width=CW):
        return _dot(u_s[...], win_ref[:, col0:col0 + width])

    c_q, s1_q, s2_q = tab_ref[0], tab_ref[1], tab_ref[2]
    inv_scale = float(HEAD_DIM) ** 0.5
    cur = slice(ATT_BLOCK, ATT_BLOCK + T)

    @task("kv", m=128, v=150)
    def _():
        c_k, s1_k, s2_k = c_q * inv_scale, s1_q * inv_scale, s2_q * inv_scale
        k = proj(D, kvw)
        v = proj(D + kvw, kvw)
        for kc in range(kvw // LANES):
            sl = slice(kc * LANES, (kc + 1) * LANES)
            for (head, var), val in _head_variants(_rope(k[:, sl], c_k, s1_k, s2_k), kc).items():
                k_s[2 * head + var, cur, :] = val.astype(BF16)
            for (head, var), val in _head_variants(v[:, sl], kc).items():
                v_s[2 * head + var, cur, :] = val.astype(BF16)

    def make_q(j):
        @task(f"q{j}", m=256, v=100)
        def _():
            q = proj(2 * j * LANES, 2 * LANES)
            for c in range(2):
                sl = slice((2 * j + c) * LANES, (2 * j + c + 1) * LANES)
                q_s[:, sl] = _rope(q[:, c * LANES:(c + 1) * LANES], c_q, s1_q, s2_q).astype(BF16)

    row = lax.broadcasted_iota(jnp.int32, (2 * ATT_BLOCK, ATT_BLOCK), 0)
    col = lax.broadcasted_iota(jnp.int32, (2 * ATT_BLOCK, ATT_BLOCK), 1)
    qpos = jnp.where(row >= ATT_BLOCK, row - ATT_BLOCK, row)
    from_prev = col > qpos
    w_prev = from_prev[0:ATT_BLOCK].astype(F32)
    w_cur = 1.0 - w_prev

    def make_attn(b, j):
        r0 = b * ATT_BLOCK
        slot = (j * nblk + b) % 2

        @task(f"qk{b}_{j}", deps=("kv", f"q{j}"), m=128, v=40)
        def _():
            qst = jnp.concatenate(
                [q_s[r0:r0 + ATT_BLOCK, (2 * j) * LANES:(2 * j + 1) * LANES],
                 q_s[r0:r0 + ATT_BLOCK, (2 * j + 1) * LANES:(2 * j + 2) * LANES]], axis=0)
            for var in range(2):
                kcat = k_s[2 * j + var, r0:r0 + 2 * ATT_BLOCK, :]
                s = _dot_nt(qst, kcat)
                s_prev = s[:, 0:ATT_BLOCK]
                if b == 0:
                    s_prev = jnp.where(si > 0, s_prev, NEG_INF)
                sc_s[slot, var] = jnp.where(from_prev, s_prev, s[:, ATT_BLOCK:])

        @task(f"sm{b}_{j}", deps=(f"qk{b}_{j}",), v=220)
        def _():
            for var in range(2):
                for half in range(2):
                    rows = slice(half * ATT_BLOCK, (half + 1) * ATT_BLOCK)
                    sc = sc_s[slot, var, rows, :]
                    sink = sink_ref[4 * j + 2 * half + var]
                    m = jnp.maximum(jnp.max(sc, axis=-1, keepdims=True), sink)
                    p = jnp.exp(sc - m)
                    denom = jnp.sum(p, axis=-1, keepdims=True) + jnp.exp(sink - m)
                    p = p * (1.0 / denom)
                    pw_s[slot, var, rows, :] = jnp.concatenate(
                        [p * w_prev, p * w_cur], axis=1).astype(BF16)

        @task(f"pv{b}_{j}", deps=(f"sm{b}_{j}",), m=128, v=20)
        def _():
            o = None
            for var in range(2):
                vcat = v_s[2 * j + var, r0:r0 + 2 * ATT_BLOCK, :]
                ov = _dot(pw_s[slot, var], vcat)
                o = ov if o is None else o + ov
            ao_s[r0:r0 + ATT_BLOCK, (2 * j) * LANES:(2 * j + 1) * LANES] = (
                o[0:ATT_BLOCK].astype(BF16))
            ao_s[r0:r0 + ATT_BLOCK, (2 * j + 1) * LANES:(2 * j + 2) * LANES] = (
                o[ATT_BLOCK:].astype(BF16))

    units = [(b, j) for j in range(N_KV_HEADS) for b in range(nblk)]
    for j in range(N_KV_HEADS):
        make_q(j)
    for b, j in units:
        make_attn(b, j)
    all_att = tuple(f"pv{b}_{j}" for b, j in units)

    def make_proj(name, dst, row0, col0, c):
        @task(f"{name}{c}", m=256)
        def _():
            dst[row0:row0 + T, cols(c)] = proj(col0 + c * CW)

    def make_gate(name, dst, idx, c):
        @task(f"{name}{c}", m=256, v=60)
        def _():
            dst[:, cols(c)] = jax.nn.sigmoid(proj(gate0 + idx * D + c * CW))

    for c in range(nchunk):
        make_proj("xr", xb_s, CONV_CARRY, base, c)
        make_proj("yr", y_s, 0, base + D, c)
        make_proj("pp", pb_s, P, base + 2 * D, c)
        make_gate("g0_", g0_s, 0, c)
        make_gate("g1_", g1_s, 1, c)
        make_gate("g2_", g2_s, 2, c)

    def make_conv(c):
        @task(f"conv{c}", deps=(f"xr{c}",), v=330)
        def _():
            sl = cols(c)
            xall = xb_s[:, sl]
            xc = convb_ref[:, sl]
            for tap in range(CONV_WIDTH):
                lag = CONV_WIDTH - 1 - tap
                xlag = pltpu.roll(xall, lag, 0) if lag else xall
                xc = xc + xlag[CONV_CARRY:, :] * convw_ref[tap:tap + 1, sl]
            xc_s[:, sl] = xc
            xb_s[0:CONV_CARRY, sl] = xb_s[T:T + CONV_CARRY, sl]

        @task(f"gm{c}", deps=(f"conv{c}",), m=64, v=16)
        def _():
            for lc in range(c * CW // LANES, (c + 1) * CW // LANES):
                ls = slice(lc * LANES, (lc + 1) * LANES)
                res = _dot(xc_s[:, ls].astype(BF16), wg_ref[lc])
                ga_s[:, ls] = res[:, 0:LANES]
                gi_s[:, ls] = res[:, LANES:]

    def make_pool(g):
        sl = cols(g)
        w = POOL_WINDOWS[g]

        @task(f"pool{g}", deps=(f"pp{g}",), v=180)
        def _():
            src, dst, lvl = pb_s, pa_s, 1
            while 2 * lvl < w:
                r = 8 * ((lvl).bit_length())
                dst[r:P + T, sl] = src[r:P + T, sl] + src[r - lvl:P + T - lvl, sl]
                src, dst = dst, (pc_s if dst is pa_s else pa_s)
                lvl *= 2
            ssum = src[P:P + T, sl] + src[P - lvl:P + T - lvl, sl]
            tpos = (si * T + lax.broadcasted_iota(jnp.int32, (T, 1), 0) + 1).astype(F32)
            inv_cnt = 1.0 / jnp.minimum(tpos, float(w))
            pd_s[:, sl] = (ssum * inv_cnt - pb_s[P:P + T, sl]).astype(BF16)
            pb_s[0:P, sl] = pb_s[T:T + P, sl]

        @task(f"pm{g}", deps=(f"pool{g}",), m=64, v=16)
        def _():
            mixed = _dot(pd_s[:, sl], wpg_ref[g])
            po_s[:, sl] = (mixed * psc_ref[:, sl]).astype(BF16)

    for c in range(nchunk):
        make_conv(c)
        make_pool(c)
    all_pool = tuple(f"pm{g}" for g in range(nchunk))

    lam = lam_ref[...]
    ls8 = LRU_C * (jnp.minimum(lam, 0.0) - jnp.log1p(jnp.exp(-jnp.abs(lam))))
    rowi = lax.broadcasted_iota(jnp.int32, (SUBLANES, CW), 0)
    hcar = {}
    nstep = T // (2 * SUBLANES)

    def scan_rows(r0, sl, hprev):
        rows = slice(r0, r0 + SUBLANES)
        r = jax.nn.sigmoid(ga_s[rows, sl] + ba_ref[:, sl])
        i = jax.nn.sigmoid(gi_s[rows, sl] + bi_ref[:, sl])
        log_a = r * ls8[:, sl]
        a = jnp.exp(log_a)
        y = jnp.tanh(-log_a) * (a * a + 1.0)
        mult = jnp.where(y > 0.0, y * lax.rsqrt(y), 0.0)
        uu = mult * (i * xc_s[rows, sl])
        for kk in (1, 2, 4):
            keep = rowi >= kk
            a_sh = jnp.where(keep, pltpu.roll(a, kk, 0), 1.0)
            u_sh = pltpu.roll(uu, kk, 0) * keep.astype(F32)
            uu = a * u_sh + uu
            a = a * a_sh
        hrows = a * hprev + uu
        hlast = jnp.broadcast_to(hrows[SUBLANES - 1:SUBLANES, :], (SUBLANES, CW))
        return hrows * jax.nn.gelu(y_s[rows, sl]), hlast

    def make_scan(c, n):
        deps = (f"gm{c}", f"yr{c}") if n == 0 else (f"scan{c}_{n - 1}",)

        @task(f"scan{c}_{n}", deps=deps, v=70)
        def _():
            sl = cols(c)
            r0 = n * 2 * SUBLANES
            h = hst_s[:, sl] if n == 0 else hcar[c]
            if "epilogue" in anchor:
                h = _after_all(h, anchor.pop("epilogue"))
            out0, h = scan_rows(r0, sl, h)
            out1, h = scan_rows(r0 + SUBLANES, sl, h)
            ro_s[r0:r0 + 2 * SUBLANES, sl] = jnp.concatenate([out0, out1], axis=0).astype(BF16)
            if n == nstep - 1:
                hst_s[:, sl] = h
            else:
                hcar[c] = h

    for c in range(nchunk):
        for n in range(nstep):
            make_scan(c, n)
    all_scan = tuple(f"scan{c}_{nstep - 1}" for c in range(nchunk))

    def make_branch(name, dst, gate, gname, src, w_ref, deps, c):
        @task(f"{name}{c}", deps=deps + (f"{gname}{c}",), m=256, v=16)
        def _():
            dst[:, cols(c)] = gate[:, cols(c)] * _dot(src[...], w_ref[:, cols(c)])

    def make_merge(c):
        @task(f"merge{c}", deps=all_scan + (f"g1_{c}", f"abr{c}", f"pbr{c}"), m=256, v=50)
        def _():
            sl = cols(c)
            rnn = g1_s[:, sl] * _dot(ro_s[...], wrnn_ref[:, sl])
            merged = (acc_s[:, sl] + rnn) + p2_s[:, sl]
            if c == nchunk - 1:
                merged = _after_all(merged, anchor.pop("prenorm_next"))
            mg_s[:, sl] = merged.astype(BF16)

    for c in range(nchunk):
        make_branch("abr", acc_s, g0_s, "g0_", ao_s, wattn_ref, all_att, c)
        make_branch("pbr", p2_s, g2_s, "g2_", po_s, wpool_ref, all_pool, c)
        make_merge(c)

    MXU_GAP = 640
    fillers = [f"xr{0}"]
    for c in range(1, nchunk):
        fillers += [f"q{c}", f"xr{c}", f"yr{c - 1}"]
    fillers += [f"yr{nchunk - 1}"] + [f"pp{c}" for c in range(nchunk)]
    fillers += [f"g{i}_{c}" for i in (0, 2, 1) for c in range(nchunk)]
    lead = 2
    main = [f"wo{0}", "kv", f"wo{1}", "q0"] + [f"wo{c}" for c in range(2, nchunk)] + fillers[:lead]
    rest = fillers[lead:]
    for n, (b, j) in enumerate(units):
        main.append(f"qk{b}_{j}")
        main += rest[n * len(rest) // len(units):(n + 1) * len(rest) // len(units)]
    main += [f"abr{c}" for c in range(nchunk)] + [f"pbr{c}" for c in range(nchunk)]
    n_before_merge = len(main)
    main += [f"merge{c}" for c in range(nchunk)]
    follow = {f"qk{b}_{j}": (f"sm{b}_{j}", f"pv{b}_{j}") for b, j in units}
    follow.update({f"xr{c}": (f"conv{c}", f"gm{c}") for c in range(nchunk)})
    follow.update({f"abr{c}": (f"pool{c}", f"pm{c}") for c in range(nchunk)})
    scan_next = [0] * nchunk
    done = []
    spent = [0]
    deferred = []

    def emit(name):
        if name in done:
            return
        fn, deps, m, v = tasks[name]
        for d in deps:
            emit(d)
        fn()
        done.append(name)
        spent[0] += m

    def scans_left():
        return sum(nstep - s for s in scan_next)

    def make_wout(c):
        @task(f"wo{c}", m=256)
        def _():
            mix_s[:, cols(c)] = _dot(mg_s[...], wout_ref[:, cols(c)])

    for c in range(nchunk):
        make_wout(c)

    @task("epilogue", deps=tuple(f"wo{c}" for c in range(nchunk)), v=250)
    def _():
        out = hprev_ref[...] + _rms(mix_s[...], gpost_ref[...])
        o_ref[...] = out
        anchor["epilogue"] = out

    @task("prenorm_next", v=200)
    def _():
        u_next = _rms(hnext_ref[...], gpre_ref[...])
        u_s[...] = u_next.astype(BF16)
        anchor["prenorm_next"] = u_next

    beside = {f"wo{nchunk - 1}": "epilogue", f"merge{0}": "prenorm_next"}
    for pos, name in enumerate(main):
        emit(name)
        if name in beside:
            emit(beside[name])
        for item in list(deferred):
            if spent[0] >= item[0]:
                emit(item[1])
                deferred.remove(item)
        if name in follow:
            producer, consumer = follow[name]
            emit(producer)
            deferred.append((spent[0] + MXU_GAP, consumer))
        quota = -(-scans_left() // max(1, n_before_merge - pos))
        for c in sorted(range(nchunk), key=lambda c: scan_next[c]):
            if quota > 0 and scan_next[c] < nstep and f"gm{c}" in done and f"yr{c}" in done:
                emit(f"scan{c}_{scan_next[c]}")
                scan_next[c] += 1
                quota -= 1
    assert set(done) == set(tasks), set(tasks) ^ set(done)

    k_s[:, 0:ATT_BLOCK, :] = k_s[:, T:T + ATT_BLOCK, :]
    v_s[:, 0:ATT_BLOCK, :] = v_s[:, T:T + ATT_BLOCK, :]


def _mlp_kernel(h_ref, gpre_ref, gpost_ref, wup_ref, wdown_ref, o_ref, m_s, acc_s):
    d_ff = wup_ref.shape[1]
    m_s[...] = _rms(h_ref[...], gpre_ref[...]).astype(BF16)
    for c in range(d_ff // FF_CHUNK):
        sl = slice(c * FF_CHUNK, (c + 1) * FF_CHUNK)
        up = jnp.maximum(_dot(m_s[...], wup_ref[:, sl]), 0.0)
        part = _dot((up * up).astype(BF16), wdown_ref[sl, :])
        if c == 0:
            acc_s[...] = part
        else:
            acc_s[...] = acc_s[...] + part
    o_ref[...] = h_ref[...] + _rms(acc_s[...], gpost_ref[...])


def _const_spec(shape):
    nd = len(shape)
    return pl.BlockSpec(shape, lambda *_: (0,) * nd, pipeline_mode=pl.Buffered(1))


def _mixer_call(h, tab, gpre, gpost, win, sinks, wattn, convw, convb, wg, ba, bi, lam,
                wrnn, wpg, psc, wpool, wout):
    B, S, D = h.shape
    T = MIX_TILE
    ns = S // T
    ntile = B * ns

    def tile_at(shift):
        def index_map(g):
            t = jnp.clip(g + shift, 0, ntile - 1)
            return (t // ns, t % ns, 0)
        return pl.BlockSpec((None, T, D), index_map)

    in_specs = [
        tile_at(0), tile_at(-1), tile_at(1),
        pl.BlockSpec((3, T, LANES), lambda g: (0, jnp.minimum(g, ntile - 1) % ns, 0)),
        _const_spec(gpre.shape), _const_spec(gpost.shape), _const_spec(win.shape),
        pl.BlockSpec(memory_space=pltpu.SMEM),
        _const_spec(wattn.shape), _const_spec(convw.shape), _const_spec(convb.shape),
        _const_spec(wg.shape), _const_spec(ba.shape), _const_spec(bi.shape),
        _const_spec(lam.shape), _const_spec(wrnn.shape), _const_spec(wpg.shape),
        _const_spec(psc.shape), _const_spec(wpool.shape), _const_spec(wout.shape),
    ]
    nvar = 2 * N_KV_HEADS
    scratch = [
        pltpu.VMEM((T, D), BF16),
        pltpu.VMEM((T, D), BF16),
        pltpu.VMEM((nvar, ATT_BLOCK + T, LANES), BF16),
        pltpu.VMEM((nvar, ATT_BLOCK + T, LANES), BF16),
        pltpu.VMEM((T, D), BF16),
        pltpu.VMEM((CONV_CARRY + T, D), F32),
        pltpu.VMEM((T, D), F32),
        pltpu.VMEM((T, D), F32),
        pltpu.VMEM((T, D), F32),
        pltpu.VMEM((T, D), F32),
        pltpu.VMEM((SUBLANES, D), F32),
        pltpu.VMEM((T, D), BF16),
        pltpu.VMEM((POOL_CARRY + T, D), F32),
        pltpu.VMEM((POOL_CARRY + T, D), F32),
        pltpu.VMEM((POOL_CARRY + T, D), F32),
        pltpu.VMEM((T, D), BF16),
        pltpu.VMEM((T, D), F32),
        pltpu.VMEM((T, D), F32),
        pltpu.VMEM((T, D), F32),
        pltpu.VMEM((T, D), F32),
        pltpu.VMEM((T, D), F32),
        pltpu.VMEM((T, D), BF16),
        pltpu.VMEM((2, 2, 2 * ATT_BLOCK, ATT_BLOCK), F32),
        pltpu.VMEM((2, 2, 2 * ATT_BLOCK, 2 * ATT_BLOCK), BF16),
        pltpu.VMEM((T, D), BF16),
        pltpu.VMEM((T, D), F32),
    ]
    return pl.pallas_call(
        functools.partial(_mixer_kernel, ns),
        grid=(ntile + 1,),
        in_specs=in_specs,
        out_specs=tile_at(-1),
        out_shape=jax.ShapeDtypeStruct(h.shape, h.dtype),
        scratch_shapes=scratch,
        compiler_params=pltpu.CompilerParams(
            dimension_semantics=("arbitrary",),
            vmem_limit_bytes=VMEM_LIMIT_BYTES),
        name="mixer",
    )(h, h, h, tab, gpre, gpost, win, sinks, wattn, convw, convb, wg, ba, bi, lam, wrnn,
      wpg, psc, wpool, wout)


def _mlp_call(h, gpre, gpost, wup, wdown):
    B, S, D = h.shape
    T = MLP_TILE
    tile = pl.BlockSpec((None, T, D), lambda b, s: (b, s, 0))
    return pl.pallas_call(
        _mlp_kernel,
        grid=(B, S // T),
        in_specs=[tile, _const_spec(gpre.shape), _const_spec(gpost.shape),
                  _const_spec(wup.shape), _const_spec(wdown.shape)],
        out_specs=tile,
        out_shape=jax.ShapeDtypeStruct(h.shape, h.dtype),
        scratch_shapes=[pltpu.VMEM((T, D), BF16), pltpu.VMEM((T, D), F32)],
        compiler_params=pltpu.CompilerParams(
            dimension_semantics=("arbitrary", "arbitrary"),
            vmem_limit_bytes=VMEM_LIMIT_BYTES),
        name="mlp",
    )(h, gpre, gpost, wup, wdown)


def _rope_tables(seq):
    half = ROT_DIM // 2
    inv_freq = ROPE_THETA ** (-jnp.arange(0, ROT_DIM, 2, dtype=F32) / ROT_DIM)
    ang = jnp.arange(seq, dtype=F32)[:, None] * inv_freq[None, :]
    cos, sin = jnp.cos(ang), jnp.sin(ang)
    pad = jnp.zeros((seq, HEAD_DIM - ROT_DIM), F32)
    c = jnp.concatenate([cos, cos, pad + 1.0], axis=1)
    s1 = jnp.concatenate([-sin, jnp.zeros_like(sin), pad], axis=1)
    s2 = jnp.concatenate([jnp.zeros_like(sin), sin, pad], axis=1)
    tab = jnp.stack([c, s1, s2])
    tab = jnp.concatenate([tab] * (LANES // HEAD_DIM), axis=2)
    return tab * (float(HEAD_DIM) ** -0.5)


def _gate_weights(w_a, w_i):
    L, nb, bw, _ = w_a.shape

    def bd(w):
        w = w.reshape(L, nb // 2, 2, bw, bw)
        z = jnp.zeros_like(w[:, :, 0])
        top = jnp.concatenate([w[:, :, 0], z], axis=-1)
        bot = jnp.concatenate([z, w[:, :, 1]], axis=-1)
        return jnp.concatenate([top, bot], axis=-2)

    return jnp.concatenate([bd(w_a), bd(w_i)], axis=-1).astype(BF16)


def kernel(x, norm_mix_pre, norm_mix_post, w_in, attn_sinks, w_attn_br, conv_w, conv_b,
           w_rg_a, b_rg_a, w_rg_i, b_rg_i, lru_lambda, w_rnn_br, w_pool_groups, pool_scale,
           w_pool_br, w_out, norm_mlp_pre, norm_mlp_post, w_mlp_up, w_mlp_down):
    B, S, D = x.shape
    depth = w_in.shape[0]
    assert S % MIX_TILE == 0 and S % MLP_TILE == 0 and MIX_TILE % ATT_BLOCK == 0
    tab = _rope_tables(S)
    wg = _gate_weights(w_rg_a, w_rg_i)
    row = lambda a, l: a[l][None, :]
    h = x
    for l in range(depth):
        h = _mixer_call(
            h, tab, row(norm_mix_pre, l), row(norm_mix_post, l), w_in[l].astype(BF16),
            attn_sinks[l], w_attn_br[l].astype(BF16), conv_w[l], row(conv_b, l), wg[l],
            row(b_rg_a, l), row(b_rg_i, l), row(lru_lambda, l), w_rnn_br[l].astype(BF16),
            w_pool_groups[l].astype(BF16), row(pool_scale, l), w_pool_br[l].astype(BF16),
            w_out[l].astype(BF16))
        h = _mlp_call(h, row(norm_mlp_pre, l), row(norm_mlp_post, l),
                      w_mlp_up[l].astype(BF16), w_mlp_down[l].astype(BF16))
    return h
```

```python
import functools

import jax
import jax.numpy as jnp
from jax import lax
from jax.experimental import pallas as pl
from jax.experimental.pallas import tpu as pltpu

F32 = jnp.float32
BF16 = jnp.bfloat16

HEAD_DIM = 64
N_KV_HEADS = 4
ATT_BLOCK = 128
ROT_DIM = HEAD_DIM // 4
ROPE_THETA = 500000.0
CONV_WIDTH = 4
LRU_C = 8.0
POOL_WINDOWS = (2, 4, 8, 16)
EPS = 1e-6
NEG_INF = -1e30

LANES = 128
SUBLANES = 8
CONV_CARRY = SUBLANES
POOL_CARRY = 32
VMEM_LIMIT_BYTES = 58 * 1024 * 1024

MIX_TILE = 256
MLP_TILE = 1024
FF_CHUNK = 1024


def _dot(a, b):
    return jnp.dot(a, b, preferred_element_type=F32)


def _dot_nt(a, b):
    return lax.dot_general(a, b, (((1,), (1,)), ((), ())), preferred_element_type=F32)


def _rms(x, g):
    ms = jnp.mean(x * x, axis=-1, keepdims=True)
    return (x * lax.rsqrt(ms + EPS)) * g


def _rope(x, c, s1, s2):
    half = ROT_DIM // 2
    return x * c + pltpu.roll(x, LANES - half, 1) * s1 + pltpu.roll(x, half, 1) * s2


def _after(x, probe):
    sh = jnp.uint32(16)
    z = lax.shift_right_logical(lax.shift_right_logical(pltpu.bitcast(probe, jnp.uint32), sh), sh)
    return pltpu.bitcast(pltpu.bitcast(x, jnp.uint32) | z, F32)


def _after_all(x, y):
    acc = y[0:SUBLANES, :]
    for r in range(SUBLANES, y.shape[0], SUBLANES):
        acc = acc + y[r:r + SUBLANES, :]
    blk = acc[:, 0:LANES]
    for c in range(LANES, y.shape[1], LANES):
        blk = blk + acc[:, c:c + LANES]
    top = _after(x[0:SUBLANES, 0:LANES], blk)
    if x.shape[1] > LANES:
        top = jnp.concatenate([top, x[0:SUBLANES, LANES:]], axis=1)
    if x.shape[0] > SUBLANES:
        top = jnp.concatenate([top, x[SUBLANES:, :]], axis=0)
    return top


def _head_variants(x, kc):
    lane = lax.broadcasted_iota(jnp.int32, x.shape, 1)
    lo = lane < HEAD_DIM
    xs = pltpu.roll(x, HEAD_DIM, 1)
    zero = jnp.zeros_like(x)
    return {
        (2 * kc, 0): jnp.where(lo, x, zero),
        (2 * kc, 1): jnp.where(lo, zero, xs),
        (2 * kc + 1, 0): jnp.where(lo, xs, zero),
        (2 * kc + 1, 1): jnp.where(lo, zero, x),
    }


def _mixer_kernel(seq_tiles, h_ref, hprev_ref, hnext_ref, tab_ref, gpre_ref, gpost_ref,
                  win_ref, sink_ref, wattn_ref,
                  convw_ref, convb_ref, wg_ref, ba_ref, bi_ref, lam_ref, wrnn_ref,
                  wpg_ref, psc_ref, wpool_ref, wout_ref, o_ref,
                  u_s, q_s, k_s, v_s, ao_s, xb_s, xc_s, ga_s, gi_s, y_s, hst_s, ro_s,
                  pb_s, pa_s, pc_s, po_s, acc_s, g0_s, g1_s, g2_s, p2_s, mg_s,
                  sc_s, pw_s, pd_s, mix_s):
    g = pl.program_id(0)
    si = lax.rem(g, seq_tiles)
    T, D = h_ref.shape
    nblk = T // ATT_BLOCK

    @pl.when(g == 0)
    def _():
        u_s[...] = _rms(h_ref[...], gpre_ref[...]).astype(BF16)
        mg_s[...] = jnp.zeros(mg_s.shape, BF16)

    @pl.when(si == 0)
    def _():
        k_s[:, 0:ATT_BLOCK, :] = jnp.zeros((2 * N_KV_HEADS, ATT_BLOCK, LANES), BF16)
        v_s[:, 0:ATT_BLOCK, :] = jnp.zeros((2 * N_KV_HEADS, ATT_BLOCK, LANES), BF16)
        xb_s[0:CONV_CARRY, :] = jnp.zeros((CONV_CARRY, D), F32)
        pb_s[0:POOL_CARRY, :] = jnp.zeros((POOL_CARRY, D), F32)
        hst_s[...] = jnp.zeros(hst_s.shape, F32)

    kvw = N_KV_HEADS * HEAD_DIM
    base = D + 2 * kvw
    gate0 = base + 3 * D
    P = POOL_CARRY
    CW = D // len(POOL_WINDOWS)
    nchunk = D // CW
    tasks = {}
    anchor = {}

    def task(name, deps=(), m=0, v=0):
        def reg(fn):
            tasks[name] = (fn, tuple(deps), m, v)
            return fn
        return reg

    def cols(c):
        return slice(c * CW, (c + 1) * CW)

    def proj(col0, width=CW):
        return _dot(u_s[...], win_ref[:, col0:col0 + width])

    c_q, s1_q, s2_q = tab_ref[0], tab_ref[1], tab_ref[2]
    inv_scale = float(HEAD_DIM) ** 0.5
    cur = slice(ATT_BLOCK, ATT_BLOCK + T)

    @task("kv", m=128, v=150)
    def _():
        c_k, s1_k, s2_k = c_q * inv_scale, s1_q * inv_scale, s2_q * inv_scale
        k = proj(D, kvw)
        v = proj(D + kvw, kvw)
        for kc in range(kvw // LANES):
            sl = slice(kc * LANES, (kc + 1) * LANES)
            for (head, var), val in _head_variants(_rope(k[:, sl], c_k, s1_k, s2_k), kc).items():
                k_s[2 * head + var, cur, :] = val.astype(BF16)
            for (head, var), val in _head_variants(v[:, sl], kc).items():
                v_s[2 * head + var, cur, :] = val.astype(BF16)

    def make_q(j):
        @task(f"q{j}", m=256, v=100)
        def _():
            q = proj(2 * j * LANES, 2 * LANES)
            for c in range(2):
                sl = slice((2 * j + c) * LANES, (2 * j + c + 1) * LANES)
                q_s[:, sl] = _rope(q[:, c * LANES:(c + 1) * LANES], c_q, s1_q, s2_q).astype(BF16)

    row = lax.broadcasted_iota(jnp.int32, (2 * ATT_BLOCK, ATT_BLOCK), 0)
    col = lax.broadcasted_iota(jnp.int32, (2 * ATT_BLOCK, ATT_BLOCK), 1)
    qpos = jnp.where(row >= ATT_BLOCK, row - ATT_BLOCK, row)
    from_prev = col > qpos
    top = row < ATT_BLOCK

    def make_attn(b, j):
        r0 = b * ATT_BLOCK
        slot = (j * nblk + b) % 2

        @task(f"qk{b}_{j}", deps=("kv", f"q{j}"), m=128, v=40)
        def _():
            qst = jnp.concatenate(
                [q_s[r0:r0 + ATT_BLOCK, (2 * j) * LANES:(2 * j + 1) * LANES],
                 q_s[r0:r0 + ATT_BLOCK, (2 * j + 1) * LANES:(2 * j + 2) * LANES]], axis=0)
            for var in range(2):
                kcat = k_s[2 * j + var, r0:r0 + 2 * ATT_BLOCK, :]
                s = _dot_nt(qst, kcat)
                s_prev = s[:, 0:ATT_BLOCK]
                if b == 0:
                    s_prev = jnp.where(si > 0, s_prev, NEG_INF)
                sc_s[slot, var] = jnp.where(from_prev, s_prev, s[:, ATT_BLOCK:])

        @task(f"sm{b}_{j}", deps=(f"qk{b}_{j}",), v=220)
        def _():
            for var in range(2):
                sc = sc_s[slot, var]
                sink = jnp.where(top, sink_ref[4 * j + var], sink_ref[4 * j + 2 + var])[:, 0:1]
                m = jnp.maximum(jnp.max(sc, axis=-1, keepdims=True), sink)
                p = jnp.exp(sc - m)
                denom = jnp.sum(p, axis=-1, keepdims=True) + jnp.exp(sink - m)
                p = p * (1.0 / denom)
                zero = jnp.zeros_like(p)
                pw_s[slot, var] = jnp.concatenate(
                    [jnp.where(from_prev, p, zero), jnp.where(from_prev, zero, p)],
                    axis=1).astype(BF16)

        @task(f"pv{b}_{j}", deps=(f"sm{b}_{j}",), m=128, v=20)
        def _():
            o = None
            for var in range(2):
                vcat = v_s[2 * j + var, r0:r0 + 2 * ATT_BLOCK, :]
                ov = _dot(pw_s[slot, var], vcat)
                o = ov if o is None else o + ov
            ao_s[r0:r0 + ATT_BLOCK, (2 * j) * LANES:(2 * j + 1) * LANES] = (
                o[0:ATT_BLOCK].astype(BF16))
            ao_s[r0:r0 + ATT_BLOCK, (2 * j + 1) * LANES:(2 * j + 2) * LANES] = (
                o[ATT_BLOCK:].astype(BF16))

    units = [(b, j) for j in range(N_KV_HEADS) for b in range(nblk)]
    for j in range(N_KV_HEADS):
        make_q(j)
    for b, j in units:
        make_attn(b, j)
    all_att = tuple(f"pv{b}_{j}" for b, j in units)

    def make_proj(name, dst, row0, col0, c):
        @task(f"{name}{c}", m=256)
        def _():
            dst[row0:row0 + T, cols(c)] = proj(col0 + c * CW)

    def make_gate(name, dst, idx, c):
        @task(f"{name}{c}", m=256, v=60)
        def _():
            dst[:, cols(c)] = jax.nn.sigmoid(proj(gate0 + idx * D + c * CW))

    for c in range(nchunk):
        make_proj("xr", xb_s, CONV_CARRY, base, c)
        make_proj("yr", y_s, 0, base + D, c)
        make_proj("pp", pb_s, P, base + 2 * D, c)
        make_gate("g0_", g0_s, 0, c)
        make_gate("g1_", g1_s, 1, c)
        make_gate("g2_", g2_s, 2, c)

    def make_conv(c):
        @task(f"conv{c}", deps=(f"xr{c}",), v=330)
        def _():
            sl = cols(c)
            xall = xb_s[:, sl]
            xc = convb_ref[:, sl]
            for tap in range(CONV_WIDTH):
                lag = CONV_WIDTH - 1 - tap
                xlag = pltpu.roll(xall, lag, 0) if lag else xall
                xc = xc + xlag[CONV_CARRY:, :] * convw_ref[tap:tap + 1, sl]
            xc_s[:, sl] = xc
            xb_s[0:CONV_CARRY, sl] = xb_s[T:T + CONV_CARRY, sl]

        @task(f"gm{c}", deps=(f"conv{c}",), m=64, v=16)
        def _():
            for lc in range(c * CW // LANES, (c + 1) * CW // LANES):
                ls = slice(lc * LANES, (lc + 1) * LANES)
                res = _dot(xc_s[:, ls].astype(BF16), wg_ref[lc])
                ga_s[:, ls] = res[:, 0:LANES]
                gi_s[:, ls] = res[:, LANES:]

    def make_pool(g):
        sl = cols(g)
        w = POOL_WINDOWS[g]

        @task(f"pool{g}", deps=(f"pp{g}",), v=180)
        def _():
            src, dst, lvl = pb_s, pa_s, 1
            while 2 * lvl < w:
                r = 8 * ((lvl).bit_length())
                dst[r:P + T, sl] = src[r:P + T, sl] + src[r - lvl:P + T - lvl, sl]
                src, dst = dst, (pc_s if dst is pa_s else pa_s)
                lvl *= 2
            ssum = src[P:P + T, sl] + src[P - lvl:P + T - lvl, sl]
            tpos = (si * T + lax.broadcasted_iota(jnp.int32, (T, 1), 0) + 1).astype(F32)
            inv_cnt = 1.0 / jnp.minimum(tpos, float(w))
            pd_s[:, sl] = (ssum * inv_cnt - pb_s[P:P + T, sl]).astype(BF16)
            pb_s[0:P, sl] = pb_s[T:T + P, sl]

        @task(f"pm{g}", deps=(f"pool{g}",), m=64, v=16)
        def _():
            mixed = _dot(pd_s[:, sl], wpg_ref[g])
            po_s[:, sl] = (mixed * psc_ref[:, sl]).astype(BF16)

    for c in range(nchunk):
        make_conv(c)
        make_pool(c)
    all_pool = tuple(f"pm{g}" for g in range(nchunk))

    lam = lam_ref[...]
    ls8 = LRU_C * (jnp.minimum(lam, 0.0) - jnp.log1p(jnp.exp(-jnp.abs(lam))))
    rowi = lax.broadcasted_iota(jnp.int32, (SUBLANES, CW), 0)
    hcar = {}
    nstep = T // (2 * SUBLANES)

    def scan_rows(r0, sl, hprev):
        rows = slice(r0, r0 + SUBLANES)
        r = jax.nn.sigmoid(ga_s[rows, sl] + ba_ref[:, sl])
        i = jax.nn.sigmoid(gi_s[rows, sl] + bi_ref[:, sl])
        log_a = r * ls8[:, sl]
        a = jnp.exp(log_a)
        y = jnp.tanh(-log_a) * (a * a + 1.0)
        mult = jnp.where(y > 0.0, y * lax.rsqrt(y), 0.0)
        uu = mult * (i * xc_s[rows, sl])
        for kk in (1, 2, 4):
            keep = rowi >= kk
            a_sh = jnp.where(keep, pltpu.roll(a, kk, 0), 1.0)
            u_sh = jnp.where(keep, pltpu.roll(uu, kk, 0), 0.0)
            uu = a * u_sh + uu
            a = a * a_sh
        hrows = a * hprev + uu
        hlast = jnp.broadcast_to(hrows[SUBLANES - 1:SUBLANES, :], (SUBLANES, CW))
        return hrows * jax.nn.gelu(y_s[rows, sl]), hlast

    def make_scan(c, n):
        deps = (f"gm{c}", f"yr{c}") if n == 0 else (f"scan{c}_{n - 1}",)

        @task(f"scan{c}_{n}", deps=deps, v=70)
        def _():
            sl = cols(c)
            r0 = n * 2 * SUBLANES
            h = hst_s[:, sl] if n == 0 else hcar[c]
            if "epilogue" in anchor:
                h = _after_all(h, anchor.pop("epilogue"))
            out0, h = scan_rows(r0, sl, h)
            out1, h = scan_rows(r0 + SUBLANES, sl, h)
            ro_s[r0:r0 + 2 * SUBLANES, sl] = jnp.concatenate([out0, out1], axis=0).astype(BF16)
            if n == nstep - 1:
                hst_s[:, sl] = h
            else:
                hcar[c] = h

    for c in range(nchunk):
        for n in range(nstep):
            make_scan(c, n)
    all_scan = tuple(f"scan{c}_{nstep - 1}" for c in range(nchunk))

    def make_branch(name, dst, gate, gname, src, w_ref, deps, c):
        @task(f"{name}{c}", deps=deps + (f"{gname}{c}",), m=256, v=16)
        def _():
            dst[:, cols(c)] = gate[:, cols(c)] * _dot(src[...], w_ref[:, cols(c)])

    def make_merge(c):
        @task(f"merge{c}", deps=all_scan + (f"g1_{c}", f"abr{c}", f"pbr{c}"), m=256, v=50)
        def _():
            sl = cols(c)
            rnn = g1_s[:, sl] * _dot(ro_s[...], wrnn_ref[:, sl])
            merged = (acc_s[:, sl] + rnn) + p2_s[:, sl]
            if c == nchunk - 1:
                merged = _after_all(merged, anchor.pop("prenorm_next"))
            mg_s[:, sl] = merged.astype(BF16)

    for c in range(nchunk):
        make_branch("abr", acc_s, g0_s, "g0_", ao_s, wattn_ref, all_att, c)
        make_branch("pbr", p2_s, g2_s, "g2_", po_s, wpool_ref, all_pool, c)
        make_merge(c)

    MXU_GAP = 640
    fillers = [f"xr{0}"]
    for c in range(1, nchunk):
        fillers += [f"q{c}", f"xr{c}", f"yr{c - 1}"]
    fillers += [f"yr{nchunk - 1}"] + [f"pp{c}" for c in range(nchunk)]
    fillers += [f"g{i}_{c}" for i in (0, 2, 1) for c in range(nchunk)]
    lead = 2
    main = [f"wo{0}", "kv", f"wo{1}", "q0"] + [f"wo{c}" for c in range(2, nchunk)] + fillers[:lead]
    rest = fillers[lead:]
    for n, (b, j) in enumerate(units):
        main.append(f"qk{b}_{j}")
        main += rest[n * len(rest) // len(units):(n + 1) * len(rest) // len(units)]
    main += [f"abr{c}" for c in range(nchunk)] + [f"pbr{c}" for c in range(nchunk)]
    n_before_merge = len(main)
    main += [f"merge{c}" for c in range(nchunk)]
    follow = {f"qk{b}_{j}": (f"sm{b}_{j}", f"pv{b}_{j}") for b, j in units}
    follow.update({f"xr{c}": (f"conv{c}", f"gm{c}") for c in range(nchunk)})
    follow.update({f"abr{c}": (f"pool{c}", f"pm{c}") for c in range(nchunk)})
    scan_next = [0] * nchunk
    done = []
    spent = [0]
    deferred = []

    def emit(name):
        if name in done:
            return
        fn, deps, m, v = tasks[name]
        for d in deps:
            emit(d)
        fn()
        done.append(name)
        spent[0] += m

    def scans_left():
        return sum(nstep - s for s in scan_next)

    def make_wout(c):
        @task(f"wo{c}", m=256)
        def _():
            mix_s[:, cols(c)] = _dot(mg_s[...], wout_ref[:, cols(c)])

    for c in range(nchunk):
        make_wout(c)

    @task("epilogue", deps=tuple(f"wo{c}" for c in range(nchunk)), v=250)
    def _():
        out = hprev_ref[...] + _rms(mix_s[...], gpost_ref[...])
        o_ref[...] = out
        anchor["epilogue"] = out

    @task("prenorm_next", v=200)
    def _():
        u_next = _rms(hnext_ref[...], gpre_ref[...])
        u_s[...] = u_next.astype(BF16)
        anchor["prenorm_next"] = u_next

    beside = {f"wo{nchunk - 1}": "epilogue", f"merge{0}": "prenorm_next"}
    for pos, name in enumerate(main):
        emit(name)
        if name in beside:
            emit(beside[name])
        for item in list(deferred):
            if spent[0] >= item[0]:
                emit(item[1])
                deferred.remove(item)
        if name in follow:
            producer, consumer = follow[name]
            emit(producer)
            deferred.append((spent[0] + MXU_GAP, consumer))
        quota = -(-scans_left() // max(1, n_before_merge - pos))
        for c in sorted(range(nchunk), key=lambda c: scan_next[c]):
            if quota > 0 and scan_next[c] < nstep and f"gm{c}" in done and f"yr{c}" in done:
                emit(f"scan{c}_{scan_next[c]}")
                scan_next[c] += 1
                quota -= 1
    assert set(done) == set(tasks), set(tasks) ^ set(done)

    k_s[:, 0:ATT_BLOCK, :] = k_s[:, T:T + ATT_BLOCK, :]
    v_s[:, 0:ATT_BLOCK, :] = v_s[:, T:T + ATT_BLOCK, :]


def _mlp_kernel(h_ref, gpre_ref, gpost_ref, wup_ref, wdown_ref, o_ref, m_s, acc_s):
    d_ff = wup_ref.shape[1]
    m_s[...] = _rms(h_ref[...], gpre_ref[...]).astype(BF16)
    for c in range(d_ff // FF_CHUNK):
        sl = slice(c * FF_CHUNK, (c + 1) * FF_CHUNK)
        up = jnp.maximum(_dot(m_s[...], wup_ref[:, sl]), 0.0)
        part = _dot((up * up).astype(BF16), wdown_ref[sl, :])
        if c == 0:
            acc_s[...] = part
        else:
            acc_s[...] = acc_s[...] + part
    o_ref[...] = h_ref[...] + _rms(acc_s[...], gpost_ref[...])


def _const_spec(shape):
    nd = len(shape)
    return pl.BlockSpec(shape, lambda *_: (0,) * nd, pipeline_mode=pl.Buffered(1))


def _mixer_call(h, tab, gpre, gpost, win, sinks, wattn, convw, convb, wg, ba, bi, lam,
                wrnn, wpg, psc, wpool, wout):
    B, S, D = h.shape
    T = MIX_TILE
    ns = S // T
    ntile = B * ns

    def tile_at(shift):
        def index_map(g):
            t = jnp.clip(g + shift, 0, ntile - 1)
            return (t // ns, t % ns, 0)
        return pl.BlockSpec((None, T, D), index_map)

    in_specs = [
        tile_at(0), tile_at(-1), tile_at(1),
        pl.BlockSpec((3, T, LANES), lambda g: (0, jnp.minimum(g, ntile - 1) % ns, 0)),
        _const_spec(gpre.shape), _const_spec(gpost.shape), _const_spec(win.shape),
        pl.BlockSpec(memory_space=pltpu.SMEM),
        _const_spec(wattn.shape), _const_spec(convw.shape), _const_spec(convb.shape),
        _const_spec(wg.shape), _const_spec(ba.shape), _const_spec(bi.shape),
        _const_spec(lam.shape), _const_spec(wrnn.shape), _const_spec(wpg.shape),
        _const_spec(psc.shape), _const_spec(wpool.shape), _const_spec(wout.shape),
    ]
    nvar = 2 * N_KV_HEADS
    scratch = [
        pltpu.VMEM((T, D), BF16),
        pltpu.VMEM((T, D), BF16),
        pltpu.VMEM((nvar, ATT_BLOCK + T, LANES), BF16),
        pltpu.VMEM((nvar, ATT_BLOCK + T, LANES), BF16),
        pltpu.VMEM((T, D), BF16),
        pltpu.VMEM((CONV_CARRY + T, D), F32),
        pltpu.VMEM((T, D), F32),
        pltpu.VMEM((T, D), F32),
        pltpu.VMEM((T, D), F32),
        pltpu.VMEM((T, D), F32),
        pltpu.VMEM((SUBLANES, D), F32),
        pltpu.VMEM((T, D), BF16),
        pltpu.VMEM((POOL_CARRY + T, D), F32),
        pltpu.VMEM((POOL_CARRY + T, D), F32),
        pltpu.VMEM((POOL_CARRY + T, D), F32),
        pltpu.VMEM((T, D), BF16),
        pltpu.VMEM((T, D), F32),
        pltpu.VMEM((T, D), F32),
        pltpu.VMEM((T, D), F32),
        pltpu.VMEM((T, D), F32),
        pltpu.VMEM((T, D), F32),
        pltpu.VMEM((T, D), BF16),
        pltpu.VMEM((2, 2, 2 * ATT_BLOCK, ATT_BLOCK), F32),
        pltpu.VMEM((2, 2, 2 * ATT_BLOCK, 2 * ATT_BLOCK), BF16),
        pltpu.VMEM((T, D), BF16),
        pltpu.VMEM((T, D), F32),
    ]
    return pl.pallas_call(
        functools.partial(_mixer_kernel, ns),
        grid=(ntile + 1,),
        in_specs=in_specs,
        out_specs=tile_at(-1),
        out_shape=jax.ShapeDtypeStruct(h.shape, h.dtype),
        scratch_shapes=scratch,
        compiler_params=pltpu.CompilerParams(
            dimension_semantics=("arbitrary",),
            vmem_limit_bytes=VMEM_LIMIT_BYTES),
        name="mixer",
    )(h, h, h, tab, gpre, gpost, win, sinks, wattn, convw, convb, wg, ba, bi, lam, wrnn,
      wpg, psc, wpool, wout)


def _mlp_call(h, gpre, gpost, wup, wdown):
    B, S, D = h.shape
    T = MLP_TILE
    tile = pl.BlockSpec((None, T, D), lambda b, s: (b, s, 0))
    return pl.pallas_call(
        _mlp_kernel,
        grid=(B, S // T),
        in_specs=[tile, _const_spec(gpre.shape), _const_spec(gpost.shape),
                  _const_spec(wup.shape), _const_spec(wdown.shape)],
        out_specs=tile,
        out_shape=jax.ShapeDtypeStruct(h.shape, h.dtype),
        scratch_shapes=[pltpu.VMEM((T, D), BF16), pltpu.VMEM((T, D), F32)],
        compiler_params=pltpu.CompilerParams(
            dimension_semantics=("arbitrary", "arbitrary"),
            vmem_limit_bytes=VMEM_LIMIT_BYTES),
        name="mlp",
    )(h, gpre, gpost, wup, wdown)


def _rope_tables(seq):
    inv_freq = ROPE_THETA ** (-jnp.arange(0, ROT_DIM, 2, dtype=F32) / ROT_DIM)
    ang = jnp.arange(seq, dtype=F32)[:, None] * inv_freq[None, :]
    cos, sin = jnp.cos(ang), jnp.sin(ang)
    pad = jnp.zeros((seq, HEAD_DIM - ROT_DIM), F32)
    c = jnp.concatenate([cos, cos, pad + 1.0], axis=1)
    s1 = jnp.concatenate([-sin, jnp.zeros_like(sin), pad], axis=1)
    s2 = jnp.concatenate([jnp.zeros_like(sin), sin, pad], axis=1)
    tab = jnp.stack([c, s1, s2])
    tab = jnp.concatenate([tab] * (LANES // HEAD_DIM), axis=2)
    return tab * (float(HEAD_DIM) ** -0.5)


def _gate_weights(w_a, w_i):
    L, nb, bw, _ = w_a.shape

    def bd(w):
        w = w.reshape(L, nb // 2, 2, bw, bw)
        z = jnp.zeros_like(w[:, :, 0])
        top = jnp.concatenate([w[:, :, 0], z], axis=-1)
        bot = jnp.concatenate([z, w[:, :, 1]], axis=-1)
        return jnp.concatenate([top, bot], axis=-2)

    return jnp.concatenate([bd(w_a), bd(w_i)], axis=-1).astype(BF16)


def kernel(x, norm_mix_pre, norm_mix_post, w_in, attn_sinks, w_attn_br, conv_w, conv_b,
           w_rg_a, b_rg_a, w_rg_i, b_rg_i, lru_lambda, w_rnn_br, w_pool_groups, pool_scale,
           w_pool_br, w_out, norm_mlp_pre, norm_mlp_post, w_mlp_up, w_mlp_down):
    B, S, D = x.shape
    depth = w_in.shape[0]
    assert S % MIX_TILE == 0 and S % MLP_TILE == 0 and MIX_TILE % ATT_BLOCK == 0
    tab = _rope_tables(S)
    wg = _gate_weights(w_rg_a, w_rg_i)
    row = lambda a, l: a[l][None, :]
    h = x
    for l in range(depth):
        h = _mixer_call(
            h, tab, row(norm_mix_pre, l), row(norm_mix_post, l), w_in[l].astype(BF16),
            attn_sinks[l], w_attn_br[l].astype(BF16), conv_w[l], row(conv_b, l), wg[l],
            row(b_rg_a, l), row(b_rg_i, l), row(lru_lambda, l), w_rnn_br[l].astype(BF16),
            w_pool_groups[l].astype(BF16), row(pool_scale, l), w_pool_br[l].astype(BF16),
            w_out[l].astype(BF16))
        h = _mlp_call(h, row(norm_mlp_pre, l), row(norm_mlp_post, l),
                      w_mlp_up[l].astype(BF16), w_mlp_down[l].astype(BF16))
    return h
```

```python
import functools

import jax
import jax.numpy as jnp
from jax import lax
from jax.experimental import pallas as pl
from jax.experimental.pallas import tpu as pltpu

F32 = jnp.float32
BF16 = jnp.bfloat16

HEAD_DIM = 64
N_KV_HEADS = 4
ATT_BLOCK = 128
ROT_DIM = HEAD_DIM // 4
ROPE_THETA = 500000.0
CONV_WIDTH = 4
LRU_C = 8.0
POOL_WINDOWS = (2, 4, 8, 16)
EPS = 1e-6
NEG_INF = -1e30

LANES = 128
SUBLANES = 8
CONV_CARRY = SUBLANES
POOL_CARRY = 32
VMEM_LIMIT_BYTES = 58 * 1024 * 1024

MIX_TILE = 256
MLP_TILE = 1024
FF_CHUNK = 1024


def _dot(a, b):
    return jnp.dot(a, b, preferred_element_type=F32)


def _dot_nt(a, b):
    return lax.dot_general(a, b, (((1,), (1,)), ((), ())), preferred_element_type=F32)


def _rms(x, g):
    ms = jnp.mean(x * x, axis=-1, keepdims=True)
    return (x * lax.rsqrt(ms + EPS)) * g


def _rope(x, c, s1, s2):
    half = ROT_DIM // 2
    return x * c + pltpu.roll(x, LANES - half, 1) * s1 + pltpu.roll(x, half, 1) * s2


def _after(x, probe):
    sh = jnp.uint32(16)
    z = lax.shift_right_logical(lax.shift_right_logical(pltpu.bitcast(probe, jnp.uint32), sh), sh)
    return pltpu.bitcast(pltpu.bitcast(x, jnp.uint32) | z, F32)


def _after_all(x, y):
    acc = y[0:SUBLANES, :]
    for r in range(SUBLANES, y.shape[0], SUBLANES):
        acc = acc + y[r:r + SUBLANES, :]
    blk = acc[:, 0:LANES]
    for c in range(LANES, y.shape[1], LANES):
        blk = blk + acc[:, c:c + LANES]
    top = _after(x[0:SUBLANES, 0:LANES], blk)
    if x.shape[1] > LANES:
        top = jnp.concatenate([top, x[0:SUBLANES, LANES:]], axis=1)
    if x.shape[0] > SUBLANES:
        top = jnp.concatenate([top, x[SUBLANES:, :]], axis=0)
    return top


def _head_variants(x, kc):
    lane = lax.broadcasted_iota(jnp.int32, x.shape, 1)
    lo = lane < HEAD_DIM
    xs = pltpu.roll(x, HEAD_DIM, 1)
    zero = jnp.zeros_like(x)
    return {
        (2 * kc, 0): jnp.where(lo, x, zero),
        (2 * kc, 1): jnp.where(lo, zero, xs),
        (2 * kc + 1, 0): jnp.where(lo, xs, zero),
        (2 * kc + 1, 1): jnp.where(lo, zero, x),
    }


def _mixer_kernel(seq_tiles, layer, h_ref, hprev_ref, hnext_ref, tab_ref, gpre_ref, gpost_ref,
                  win_ref, sink_ref, wattn_ref,
                  convw_ref, convb_ref, wg_ref, ba_ref, bi_ref, lam_ref, wrnn_ref,
                  wpg_ref, psc_ref, wpool_ref, wout_ref, o_ref,
                  u_s, q_s, k_s, v_s, ao_s, xb_s, xc_s, ga_s, gi_s, y_s, hst_s, ro_s,
                  pb_s, pa_s, pc_s, po_s, acc_s, g0_s, g1_s, g2_s, p2_s, mg_s,
                  sc_s, pw_s, pd_s, mix_s):
    g = pl.program_id(0)
    si = lax.rem(g, seq_tiles)
    T, D = h_ref.shape
    nblk = T // ATT_BLOCK

    @pl.when(g == 0)
    def _():
        u_s[...] = _rms(h_ref[...], gpre_ref[...]).astype(BF16)
        mg_s[...] = jnp.zeros(mg_s.shape, BF16)

    @pl.when(si == 0)
    def _():
        k_s[:, 0:ATT_BLOCK, :] = jnp.zeros((2 * N_KV_HEADS, ATT_BLOCK, LANES), BF16)
        v_s[:, 0:ATT_BLOCK, :] = jnp.zeros((2 * N_KV_HEADS, ATT_BLOCK, LANES), BF16)
        xb_s[0:CONV_CARRY, :] = jnp.zeros((CONV_CARRY, D), F32)
        pb_s[0:POOL_CARRY, :] = jnp.zeros((POOL_CARRY, D), F32)
        hst_s[...] = jnp.zeros(hst_s.shape, F32)

    kvw = N_KV_HEADS * HEAD_DIM
    base = D + 2 * kvw
    gate0 = base + 3 * D
    P = POOL_CARRY
    CW = D // len(POOL_WINDOWS)
    nchunk = D // CW
    tasks = {}
    anchor = {}

    def task(name, deps=(), m=0, v=0):
        def reg(fn):
            tasks[name] = (fn, tuple(deps), m, v)
            return fn
        return reg

    def cols(c):
        return slice(c * CW, (c + 1) * CW)

    def proj(col0, width=CW):
        return _dot(u_s[...], win_ref[:, col0:col0 + width])

    c_q, s1_q, s2_q = tab_ref[0], tab_ref[1], tab_ref[2]
    inv_scale = float(HEAD_DIM) ** 0.5
    cur = slice(ATT_BLOCK, ATT_BLOCK + T)

    @task("kv", m=128, v=150)
    def _():
        c_k, s1_k, s2_k = c_q * inv_scale, s1_q * inv_scale, s2_q * inv_scale
        k = proj(D, kvw)
        v = proj(D + kvw, kvw)
        for kc in range(kvw // LANES):
            sl = slice(kc * LANES, (kc + 1) * LANES)
            for (head, var), val in _head_variants(_rope(k[:, sl], c_k, s1_k, s2_k), kc).items():
                k_s[2 * head + var, cur, :] = val.astype(BF16)
            for (head, var), val in _head_variants(v[:, sl], kc).items():
                v_s[2 * head + var, cur, :] = val.astype(BF16)

    def make_q(j):
        @task(f"q{j}", m=256, v=100)
        def _():
            q = proj(2 * j * LANES, 2 * LANES)
            for c in range(2):
                sl = slice((2 * j + c) * LANES, (2 * j + c + 1) * LANES)
                q_s[:, sl] = _rope(q[:, c * LANES:(c + 1) * LANES], c_q, s1_q, s2_q).astype(BF16)

    row = lax.broadcasted_iota(jnp.int32, (2 * ATT_BLOCK, ATT_BLOCK), 0)
    col = lax.broadcasted_iota(jnp.int32, (2 * ATT_BLOCK, ATT_BLOCK), 1)
    qpos = jnp.where(row >= ATT_BLOCK, row - ATT_BLOCK, row)
    from_prev = col > qpos
    top = row < ATT_BLOCK

    def make_attn(b, j):
        r0 = b * ATT_BLOCK
        slot = (j * nblk + b) % 2

        @task(f"qk{b}_{j}", deps=("kv", f"q{j}"), m=128, v=40)
        def _():
            qst = jnp.concatenate(
                [q_s[r0:r0 + ATT_BLOCK, (2 * j) * LANES:(2 * j + 1) * LANES],
                 q_s[r0:r0 + ATT_BLOCK, (2 * j + 1) * LANES:(2 * j + 2) * LANES]], axis=0)
            for var in range(2):
                kcat = k_s[2 * j + var, r0:r0 + 2 * ATT_BLOCK, :]
                s = _dot_nt(qst, kcat)
                s_prev = s[:, 0:ATT_BLOCK]
                if b == 0:
                    s_prev = jnp.where(si > 0, s_prev, NEG_INF)
                sc_s[slot, var] = jnp.where(from_prev, s_prev, s[:, ATT_BLOCK:])

        @task(f"sm{b}_{j}", deps=(f"qk{b}_{j}",), v=220)
        def _():
            for var in range(2):
                sc = sc_s[slot, var]
                sink = jnp.where(top, sink_ref[layer, 4 * j + var],
                                 sink_ref[layer, 4 * j + 2 + var])[:, 0:1]
                m = jnp.maximum(jnp.max(sc, axis=-1, keepdims=True), sink)
                p = jnp.exp(sc - m)
                denom = jnp.sum(p, axis=-1, keepdims=True) + jnp.exp(sink - m)
                p = p * (1.0 / denom)
                zero = jnp.zeros_like(p)
                pw_s[slot, var] = jnp.concatenate(
                    [jnp.where(from_prev, p, zero), jnp.where(from_prev, zero, p)],
                    axis=1).astype(BF16)

        @task(f"pv{b}_{j}", deps=(f"sm{b}_{j}",), m=128, v=20)
        def _():
            o = None
            for var in range(2):
                vcat = v_s[2 * j + var, r0:r0 + 2 * ATT_BLOCK, :]
                ov = _dot(pw_s[slot, var], vcat)
                o = ov if o is None else o + ov
            ao_s[r0:r0 + ATT_BLOCK, (2 * j) * LANES:(2 * j + 1) * LANES] = (
                o[0:ATT_BLOCK].astype(BF16))
            ao_s[r0:r0 + ATT_BLOCK, (2 * j + 1) * LANES:(2 * j + 2) * LANES] = (
                o[ATT_BLOCK:].astype(BF16))

    units = [(b, j) for j in range(N_KV_HEADS) for b in range(nblk)]
    for j in range(N_KV_HEADS):
        make_q(j)
    for b, j in units:
        make_attn(b, j)
    all_att = tuple(f"pv{b}_{j}" for b, j in units)

    def make_proj(name, dst, row0, col0, c):
        @task(f"{name}{c}", m=256)
        def _():
            dst[row0:row0 + T, cols(c)] = proj(col0 + c * CW)

    def make_gate(name, dst, idx, c):
        @task(f"{name}{c}", m=256, v=60)
        def _():
            dst[:, cols(c)] = jax.nn.sigmoid(proj(gate0 + idx * D + c * CW))

    for c in range(nchunk):
        make_proj("xr", xb_s, CONV_CARRY, base, c)
        make_proj("yr", y_s, 0, base + D, c)
        make_proj("pp", pb_s, P, base + 2 * D, c)
        make_gate("g0_", g0_s, 0, c)
        make_gate("g1_", g1_s, 1, c)
        make_gate("g2_", g2_s, 2, c)

    def make_conv(c):
        @task(f"conv{c}", deps=(f"xr{c}",), v=330)
        def _():
            sl = cols(c)
            xall = xb_s[:, sl]
            xc = convb_ref[:, sl]
            for tap in range(CONV_WIDTH):
                lag = CONV_WIDTH - 1 - tap
                xlag = pltpu.roll(xall, lag, 0) if lag else xall
                xc = xc + xlag[CONV_CARRY:, :] * convw_ref[tap:tap + 1, sl]
            xc_s[:, sl] = xc
            xb_s[0:CONV_CARRY, sl] = xb_s[T:T + CONV_CARRY, sl]

        @task(f"gm{c}", deps=(f"conv{c}",), m=64, v=16)
        def _():
            for lc in range(c * CW // LANES, (c + 1) * CW // LANES):
                ls = slice(lc * LANES, (lc + 1) * LANES)
                res = _dot(xc_s[:, ls].astype(BF16), wg_ref[lc])
                ga_s[:, ls] = res[:, 0:LANES]
                gi_s[:, ls] = res[:, LANES:]

    def make_pool(g):
        sl = cols(g)
        w = POOL_WINDOWS[g]

        @task(f"pool{g}", deps=(f"pp{g}",), v=180)
        def _():
            src, dst, lvl = pb_s, pa_s, 1
            while 2 * lvl < w:
                r = 8 * ((lvl).bit_length())
                dst[r:P + T, sl] = src[r:P + T, sl] + src[r - lvl:P + T - lvl, sl]
                src, dst = dst, (pc_s if dst is pa_s else pa_s)
                lvl *= 2
            ssum = src[P:P + T, sl] + src[P - lvl:P + T - lvl, sl]
            tpos = (si * T + lax.broadcasted_iota(jnp.int32, (T, 1), 0) + 1).astype(F32)
            inv_cnt = 1.0 / jnp.minimum(tpos, float(w))
            pd_s[:, sl] = (ssum * inv_cnt - pb_s[P:P + T, sl]).astype(BF16)
            pb_s[0:P, sl] = pb_s[T:T + P, sl]

        @task(f"pm{g}", deps=(f"pool{g}",), m=64, v=16)
        def _():
            mixed = _dot(pd_s[:, sl], wpg_ref[g])
            po_s[:, sl] = (mixed * psc_ref[:, sl]).astype(BF16)

    for c in range(nchunk):
        make_conv(c)
        make_pool(c)
    all_pool = tuple(f"pm{g}" for g in range(nchunk))

    lam = lam_ref[...]
    ls8 = LRU_C * (jnp.minimum(lam, 0.0) - jnp.log1p(jnp.exp(-jnp.abs(lam))))
    rowi = lax.broadcasted_iota(jnp.int32, (SUBLANES, CW), 0)
    hcar = {}
    nstep = T // (2 * SUBLANES)

    def scan_rows(r0, sl, hprev):
        rows = slice(r0, r0 + SUBLANES)
        r = jax.nn.sigmoid(ga_s[rows, sl] + ba_ref[:, sl])
        i = jax.nn.sigmoid(gi_s[rows, sl] + bi_ref[:, sl])
        log_a = r * ls8[:, sl]
        a = jnp.exp(log_a)
        y = jnp.tanh(-log_a) * (a * a + 1.0)
        mult = jnp.where(y > 0.0, y * lax.rsqrt(y), 0.0)
        uu = mult * (i * xc_s[rows, sl])
        for kk in (1, 2, 4):
            keep = rowi >= kk
            a_sh = jnp.where(keep, pltpu.roll(a, kk, 0), 1.0)
            u_sh = jnp.where(keep, pltpu.roll(uu, kk, 0), 0.0)
            uu = a * u_sh + uu
            a = a * a_sh
        hrows = a * hprev + uu
        hlast = jnp.broadcast_to(hrows[SUBLANES - 1:SUBLANES, :], (SUBLANES, CW))
        return hrows * jax.nn.gelu(y_s[rows, sl]), hlast

    def make_scan(c, n):
        deps = (f"gm{c}", f"yr{c}") if n == 0 else (f"scan{c}_{n - 1}",)

        @task(f"scan{c}_{n}", deps=deps, v=70)
        def _():
            sl = cols(c)
            r0 = n * 2 * SUBLANES
            h = hst_s[:, sl] if n == 0 else hcar[c]
            if "epilogue" in anchor:
                h = _after_all(h, anchor.pop("epilogue"))
            out0, h = scan_rows(r0, sl, h)
            out1, h = scan_rows(r0 + SUBLANES, sl, h)
            ro_s[r0:r0 + 2 * SUBLANES, sl] = jnp.concatenate([out0, out1], axis=0).astype(BF16)
            if n == nstep - 1:
                hst_s[:, sl] = h
            else:
                hcar[c] = h

    for c in range(nchunk):
        for n in range(nstep):
            make_scan(c, n)
    all_scan = tuple(f"scan{c}_{nstep - 1}" for c in range(nchunk))

    def make_branch(name, dst, gate, gname, src, w_ref, deps, c):
        @task(f"{name}{c}", deps=deps + (f"{gname}{c}",), m=256, v=16)
        def _():
            dst[:, cols(c)] = gate[:, cols(c)] * _dot(src[...], w_ref[:, cols(c)])

    def make_merge(c):
        @task(f"merge{c}", deps=all_scan + (f"g1_{c}", f"abr{c}", f"pbr{c}"), m=256, v=50)
        def _():
            sl = cols(c)
            rnn = g1_s[:, sl] * _dot(ro_s[...], wrnn_ref[:, sl])
            merged = (acc_s[:, sl] + rnn) + p2_s[:, sl]
            if c == nchunk - 1:
                merged = _after_all(merged, anchor.pop("prenorm_next"))
            mg_s[:, sl] = merged.astype(BF16)

    for c in range(nchunk):
        make_branch("abr", acc_s, g0_s, "g0_", ao_s, wattn_ref, all_att, c)
        make_branch("pbr", p2_s, g2_s, "g2_", po_s, wpool_ref, all_pool, c)
        make_merge(c)

    MXU_GAP = 640
    fillers = [f"xr{0}"]
    for c in range(1, nchunk):
        fillers += [f"q{c}", f"xr{c}", f"yr{c - 1}"]
    fillers += [f"yr{nchunk - 1}"] + [f"pp{c}" for c in range(nchunk)]
    fillers += [f"g{i}_{c}" for i in (0, 2, 1) for c in range(nchunk)]
    lead = 2
    main = [f"wo{0}", "kv", f"wo{1}", "q0"] + [f"wo{c}" for c in range(2, nchunk)] + fillers[:lead]
    rest = fillers[lead:]
    for n, (b, j) in enumerate(units):
        main.append(f"qk{b}_{j}")
        main += rest[n * len(rest) // len(units):(n + 1) * len(rest) // len(units)]
    main += [f"abr{c}" for c in range(nchunk)] + [f"pbr{c}" for c in range(nchunk)]
    n_before_merge = len(main)
    main += [f"merge{c}" for c in range(nchunk)]
    follow = {f"qk{b}_{j}": (f"sm{b}_{j}", f"pv{b}_{j}") for b, j in units}
    follow.update({f"xr{c}": (f"conv{c}", f"gm{c}") for c in range(nchunk)})
    follow.update({f"abr{c}": (f"pool{c}", f"pm{c}") for c in range(nchunk)})
    scan_next = [0] * nchunk
    done = []
    spent = [0]
    deferred = []

    def emit(name):
        if name in done:
            return
        fn, deps, m, v = tasks[name]
        for d in deps:
            emit(d)
        fn()
        done.append(name)
        spent[0] += m

    def scans_left():
        return sum(nstep - s for s in scan_next)

    def make_wout(c):
        @task(f"wo{c}", m=256)
        def _():
            mix_s[:, cols(c)] = _dot(mg_s[...], wout_ref[:, cols(c)])

    for c in range(nchunk):
        make_wout(c)

    @task("epilogue", deps=tuple(f"wo{c}" for c in range(nchunk)), v=250)
    def _():
        out = hprev_ref[...] + _rms(mix_s[...], gpost_ref[...])
        o_ref[...] = out
        anchor["epilogue"] = out

    @task("prenorm_next", v=200)
    def _():
        u_next = _rms(hnext_ref[...], gpre_ref[...])
        u_s[...] = u_next.astype(BF16)
        anchor["prenorm_next"] = u_next

    beside = {f"wo{nchunk - 1}": "epilogue", f"merge{0}": "prenorm_next"}
    for pos, name in enumerate(main):
        emit(name)
        if name in beside:
            emit(beside[name])
        for item in list(deferred):
            if spent[0] >= item[0]:
                emit(item[1])
                deferred.remove(item)
        if name in follow:
            producer, consumer = follow[name]
            emit(producer)
            deferred.append((spent[0] + MXU_GAP, consumer))
        quota = -(-scans_left() // max(1, n_before_merge - pos))
        for c in sorted(range(nchunk), key=lambda c: scan_next[c]):
            if quota > 0 and scan_next[c] < nstep and f"gm{c}" in done and f"yr{c}" in done:
                emit(f"scan{c}_{scan_next[c]}")
                scan_next[c] += 1
                quota -= 1
    assert set(done) == set(tasks), set(tasks) ^ set(done)

    k_s[:, 0:ATT_BLOCK, :] = k_s[:, T:T + ATT_BLOCK, :]
    v_s[:, 0:ATT_BLOCK, :] = v_s[:, T:T + ATT_BLOCK, :]


def _mlp_kernel(h_ref, gpre_ref, gpost_ref, wup_ref, wdown_ref, o_ref, m_s, acc_s):
    d_ff = wup_ref.shape[1]
    m_s[...] = _rms(h_ref[...], gpre_ref[...]).astype(BF16)
    for c in range(d_ff // FF_CHUNK):
        sl = slice(c * FF_CHUNK, (c + 1) * FF_CHUNK)
        up = jnp.maximum(_dot(m_s[...], wup_ref[:, sl]), 0.0)
        part = _dot((up * up).astype(BF16), wdown_ref[sl, :])
        if c == 0:
            acc_s[...] = part
        else:
            acc_s[...] = acc_s[...] + part
    o_ref[...] = h_ref[...] + _rms(acc_s[...], gpost_ref[...])


def _layer_spec(stacked, layer):
    nd = stacked.ndim - 1
    return pl.BlockSpec((None,) + stacked.shape[1:], lambda *_: (layer,) + (0,) * nd,
                        pipeline_mode=pl.Buffered(1))


def _mixer_call(layer, h, tab, gpre, gpost, win, sinks, wattn, convw, convb, wg, ba, bi, lam,
                wrnn, wpg, psc, wpool, wout):
    B, S, D = h.shape
    T = MIX_TILE
    ns = S // T
    ntile = B * ns

    def tile_at(shift):
        def index_map(g):
            t = jnp.clip(g + shift, 0, ntile - 1)
            return (t // ns, t % ns, 0)
        return pl.BlockSpec((None, T, D), index_map)

    in_specs = [
        tile_at(0), tile_at(-1), tile_at(1),
        pl.BlockSpec((3, T, LANES), lambda g: (0, jnp.minimum(g, ntile - 1) % ns, 0)),
        _layer_spec(gpre, layer), _layer_spec(gpost, layer), _layer_spec(win, layer),
        pl.BlockSpec(memory_space=pltpu.SMEM),
        _layer_spec(wattn, layer), _layer_spec(convw, layer), _layer_spec(convb, layer),
        _layer_spec(wg, layer), _layer_spec(ba, layer), _layer_spec(bi, layer),
        _layer_spec(lam, layer), _layer_spec(wrnn, layer), _layer_spec(wpg, layer),
        _layer_spec(psc, layer), _layer_spec(wpool, layer), _layer_spec(wout, layer),
    ]
    nvar = 2 * N_KV_HEADS
    scratch = [
        pltpu.VMEM((T, D), BF16),
        pltpu.VMEM((T, D), BF16),
        pltpu.VMEM((nvar, ATT_BLOCK + T, LANES), BF16),
        pltpu.VMEM((nvar, ATT_BLOCK + T, LANES), BF16),
        pltpu.VMEM((T, D), BF16),
        pltpu.VMEM((CONV_CARRY + T, D), F32),
        pltpu.VMEM((T, D), F32),
        pltpu.VMEM((T, D), F32),
        pltpu.VMEM((T, D), F32),
        pltpu.VMEM((T, D), F32),
        pltpu.VMEM((SUBLANES, D), F32),
        pltpu.VMEM((T, D), BF16),
        pltpu.VMEM((POOL_CARRY + T, D), F32),
        pltpu.VMEM((POOL_CARRY + T, D), F32),
        pltpu.VMEM((POOL_CARRY + T, D), F32),
        pltpu.VMEM((T, D), BF16),
        pltpu.VMEM((T, D), F32),
        pltpu.VMEM((T, D), F32),
        pltpu.VMEM((T, D), F32),
        pltpu.VMEM((T, D), F32),
        pltpu.VMEM((T, D), F32),
        pltpu.VMEM((T, D), BF16),
        pltpu.VMEM((2, 2, 2 * ATT_BLOCK, ATT_BLOCK), F32),
        pltpu.VMEM((2, 2, 2 * ATT_BLOCK, 2 * ATT_BLOCK), BF16),
        pltpu.VMEM((T, D), BF16),
        pltpu.VMEM((T, D), F32),
    ]
    return pl.pallas_call(
        functools.partial(_mixer_kernel, ns, layer),
        grid=(ntile + 1,),
        in_specs=in_specs,
        out_specs=tile_at(-1),
        out_shape=jax.ShapeDtypeStruct(h.shape, h.dtype),
        scratch_shapes=scratch,
        compiler_params=pltpu.CompilerParams(
            dimension_semantics=("arbitrary",),
            vmem_limit_bytes=VMEM_LIMIT_BYTES),
        name="mixer",
    )(h, h, h, tab, gpre, gpost, win, sinks, wattn, convw, convb, wg, ba, bi, lam, wrnn,
      wpg, psc, wpool, wout)


def _mlp_call(layer, h, gpre, gpost, wup, wdown):
    B, S, D = h.shape
    T = MLP_TILE
    tile = pl.BlockSpec((None, T, D), lambda b, s: (b, s, 0))
    return pl.pallas_call(
        _mlp_kernel,
        grid=(B, S // T),
        in_specs=[tile, _layer_spec(gpre, layer), _layer_spec(gpost, layer),
                  _layer_spec(wup, layer), _layer_spec(wdown, layer)],
        out_specs=tile,
        out_shape=jax.ShapeDtypeStruct(h.shape, h.dtype),
        scratch_shapes=[pltpu.VMEM((T, D), BF16), pltpu.VMEM((T, D), F32)],
        compiler_params=pltpu.CompilerParams(
            dimension_semantics=("arbitrary", "arbitrary"),
            vmem_limit_bytes=VMEM_LIMIT_BYTES),
        name="mlp",
    )(h, gpre, gpost, wup, wdown)


def _rope_tables(seq):
    inv_freq = ROPE_THETA ** (-jnp.arange(0, ROT_DIM, 2, dtype=F32) / ROT_DIM)
    ang = jnp.arange(seq, dtype=F32)[:, None] * inv_freq[None, :]
    cos, sin = jnp.cos(ang), jnp.sin(ang)
    pad = jnp.zeros((seq, HEAD_DIM - ROT_DIM), F32)
    c = jnp.concatenate([cos, cos, pad + 1.0], axis=1)
    s1 = jnp.concatenate([-sin, jnp.zeros_like(sin), pad], axis=1)
    s2 = jnp.concatenate([jnp.zeros_like(sin), sin, pad], axis=1)
    tab = jnp.stack([c, s1, s2])
    tab = jnp.concatenate([tab] * (LANES // HEAD_DIM), axis=2)
    return tab * (float(HEAD_DIM) ** -0.5)


def _gate_weights(w_a, w_i):
    L, nb, bw, _ = w_a.shape

    def bd(w):
        w = w.reshape(L, nb // 2, 2, bw, bw)
        z = jnp.zeros_like(w[:, :, 0])
        top = jnp.concatenate([w[:, :, 0], z], axis=-1)
        bot = jnp.concatenate([z, w[:, :, 1]], axis=-1)
        return jnp.concatenate([top, bot], axis=-2)

    return jnp.concatenate([bd(w_a), bd(w_i)], axis=-1).astype(BF16)


def kernel(x, norm_mix_pre, norm_mix_post, w_in, attn_sinks, w_attn_br, conv_w, conv_b,
           w_rg_a, b_rg_a, w_rg_i, b_rg_i, lru_lambda, w_rnn_br, w_pool_groups, pool_scale,
           w_pool_br, w_out, norm_mlp_pre, norm_mlp_post, w_mlp_up, w_mlp_down):
    B, S, D = x.shape
    depth = w_in.shape[0]
    assert S % MIX_TILE == 0 and S % MLP_TILE == 0 and MIX_TILE % ATT_BLOCK == 0
    tab = _rope_tables(S)
    rows = lambda a: a[:, None, :]
    bf = lambda a: a.astype(BF16)
    mixer_params = (
        rows(norm_mix_pre), rows(norm_mix_post), bf(w_in), attn_sinks, bf(w_attn_br), conv_w,
        rows(conv_b), _gate_weights(w_rg_a, w_rg_i), rows(b_rg_a), rows(b_rg_i),
        rows(lru_lambda), bf(w_rnn_br), bf(w_pool_groups), rows(pool_scale), bf(w_pool_br),
        bf(w_out))
    mlp_params = (rows(norm_mlp_pre), rows(norm_mlp_post), bf(w_mlp_up), bf(w_mlp_down))
    h = x
    for l in range(depth):
        h = _mixer_call(l, h, tab, *mixer_params)
        h = _mlp_call(l, h, *mlp_params)
    return h
```

```python
import functools

import jax
import jax.numpy as jnp
from jax import lax
from jax.experimental import pallas as pl
from jax.experimental.pallas import tpu as pltpu

F32 = jnp.float32
BF16 = jnp.bfloat16

HEAD_DIM = 64
N_KV_HEADS = 4
ATT_BLOCK = 128
ROT_DIM = HEAD_DIM // 4
ROPE_THETA = 500000.0
CONV_WIDTH = 4
LRU_C = 8.0
POOL_WINDOWS = (2, 4, 8, 16)
EPS = 1e-6
NEG_INF = -1e30

LANES = 128
SUBLANES = 8
CONV_CARRY = SUBLANES
POOL_CARRY = 32
VMEM_LIMIT_BYTES = 58 * 1024 * 1024

MIX_TILE = 256
MLP_TILE = 1024
FF_CHUNK = 1024


def _dot(a, b):
    return jnp.dot(a, b, preferred_element_type=F32)


def _dot_nt(a, b):
    return lax.dot_general(a, b, (((1,), (1,)), ((), ())), preferred_element_type=F32)


def _rms(x, g):
    ms = jnp.mean(x * x, axis=-1, keepdims=True)
    return (x * lax.rsqrt(ms + EPS)) * g


def _rope(x, c, s1, s2):
    half = ROT_DIM // 2
    return x * c + pltpu.roll(x, LANES - half, 1) * s1 + pltpu.roll(x, half, 1) * s2


def _after(x, probe):
    sh = jnp.uint32(16)
    z = lax.shift_right_logical(lax.shift_right_logical(pltpu.bitcast(probe, jnp.uint32), sh), sh)
    return pltpu.bitcast(pltpu.bitcast(x, jnp.uint32) | z, F32)


def _after_all(x, y):
    acc = y[0:SUBLANES, :]
    for r in range(SUBLANES, y.shape[0], SUBLANES):
        acc = acc + y[r:r + SUBLANES, :]
    blk = acc[:, 0:LANES]
    for c in range(LANES, y.shape[1], LANES):
        blk = blk + acc[:, c:c + LANES]
    top = _after(x[0:SUBLANES, 0:LANES], blk)
    if x.shape[1] > LANES:
        top = jnp.concatenate([top, x[0:SUBLANES, LANES:]], axis=1)
    if x.shape[0] > SUBLANES:
        top = jnp.concatenate([top, x[SUBLANES:, :]], axis=0)
    return top


def _head_variants(x, kc):
    lane = lax.broadcasted_iota(jnp.int32, x.shape, 1)
    lo = lane < HEAD_DIM
    xs = pltpu.roll(x, HEAD_DIM, 1)
    zero = jnp.zeros_like(x)
    return {
        (2 * kc, 0): jnp.where(lo, x, zero),
        (2 * kc, 1): jnp.where(lo, zero, xs),
        (2 * kc + 1, 0): jnp.where(lo, xs, zero),
        (2 * kc + 1, 1): jnp.where(lo, zero, x),
    }


def _mixer_kernel(seq_tiles, layer, h_ref, hnext_ref, tab_ref, gpre_ref, gpost_ref,
                  win_ref, sink_ref, wattn_ref,
                  convw_ref, convb_ref, wg_ref, ba_ref, bi_ref, lam_ref, wrnn_ref,
                  wpg_ref, psc_ref, wpool_ref, wout_ref, o_ref,
                  u_s, q_s, k_s, v_s, ao_s, xb_s, xc_s, ga_s, gi_s, y_s, hst_s, ro_s,
                  pb_s, pa_s, pc_s, po_s, acc_s, g0_s, g1_s, g2_s, p2_s, mg_s,
                  sc_s, pw_s, pd_s, mix_s, hkeep_s):
    g = pl.program_id(0)
    si = lax.rem(g, seq_tiles)
    T, D = h_ref.shape
    nblk = T // ATT_BLOCK

    @pl.when(g == 0)
    def _():
        u_s[...] = _rms(h_ref[...], gpre_ref[...]).astype(BF16)
        mg_s[...] = jnp.zeros(mg_s.shape, BF16)
        hkeep_s[...] = jnp.zeros(hkeep_s.shape, F32)

    @pl.when(si == 0)
    def _():
        k_s[:, 0:ATT_BLOCK, :] = jnp.zeros((2 * N_KV_HEADS, ATT_BLOCK, LANES), BF16)
        v_s[:, 0:ATT_BLOCK, :] = jnp.zeros((2 * N_KV_HEADS, ATT_BLOCK, LANES), BF16)
        xb_s[0:CONV_CARRY, :] = jnp.zeros((CONV_CARRY, D), F32)
        pb_s[0:POOL_CARRY, :] = jnp.zeros((POOL_CARRY, D), F32)
        hst_s[...] = jnp.zeros(hst_s.shape, F32)

    kvw = N_KV_HEADS * HEAD_DIM
    base = D + 2 * kvw
    gate0 = base + 3 * D
    P = POOL_CARRY
    CW = D // len(POOL_WINDOWS)
    nchunk = D // CW
    tasks = {}
    anchor = {}

    def task(name, deps=(), m=0, v=0):
        def reg(fn):
            tasks[name] = (fn, tuple(deps), m, v)
            return fn
        return reg

    def cols(c):
        return slice(c * CW, (c + 1) * CW)

    def proj(col0, width=CW):
        return _dot(u_s[...], win_ref[:, col0:col0 + width])

    c_q, s1_q, s2_q = tab_ref[0], tab_ref[1], tab_ref[2]
    inv_scale = float(HEAD_DIM) ** 0.5
    cur = slice(ATT_BLOCK, ATT_BLOCK + T)

    @task("kv", m=128, v=150)
    def _():
        c_k, s1_k, s2_k = c_q * inv_scale, s1_q * inv_scale, s2_q * inv_scale
        k = proj(D, kvw)
        v = proj(D + kvw, kvw)
        for kc in range(kvw // LANES):
            sl = slice(kc * LANES, (kc + 1) * LANES)
            for (head, var), val in _head_variants(_rope(k[:, sl], c_k, s1_k, s2_k), kc).items():
                k_s[2 * head + var, cur, :] = val.astype(BF16)
            for (head, var), val in _head_variants(v[:, sl], kc).items():
                v_s[2 * head + var, cur, :] = val.astype(BF16)

    def make_q(j):
        @task(f"q{j}", m=256, v=100)
        def _():
            q = proj(2 * j * LANES, 2 * LANES)
            for c in range(2):
                sl = slice((2 * j + c) * LANES, (2 * j + c + 1) * LANES)
                q_s[:, sl] = _rope(q[:, c * LANES:(c + 1) * LANES], c_q, s1_q, s2_q).astype(BF16)

    row = lax.broadcasted_iota(jnp.int32, (2 * ATT_BLOCK, ATT_BLOCK), 0)
    col = lax.broadcasted_iota(jnp.int32, (2 * ATT_BLOCK, ATT_BLOCK), 1)
    qpos = jnp.where(row >= ATT_BLOCK, row - ATT_BLOCK, row)
    from_prev = col > qpos
    top = row < ATT_BLOCK

    def make_attn(b, j):
        r0 = b * ATT_BLOCK
        slot = (j * nblk + b) % 2

        @task(f"qk{b}_{j}", deps=("kv", f"q{j}"), m=128, v=40)
        def _():
            qst = jnp.concatenate(
                [q_s[r0:r0 + ATT_BLOCK, (2 * j) * LANES:(2 * j + 1) * LANES],
                 q_s[r0:r0 + ATT_BLOCK, (2 * j + 1) * LANES:(2 * j + 2) * LANES]], axis=0)
            for var in range(2):
                kcat = k_s[2 * j + var, r0:r0 + 2 * ATT_BLOCK, :]
                s = _dot_nt(qst, kcat)
                s_prev = s[:, 0:ATT_BLOCK]
                if b == 0:
                    s_prev = jnp.where(si > 0, s_prev, NEG_INF)
                sc_s[slot, var] = jnp.where(from_prev, s_prev, s[:, ATT_BLOCK:])

        @task(f"sm{b}_{j}", deps=(f"qk{b}_{j}",), v=220)
        def _():
            for var in range(2):
                sc = sc_s[slot, var]
                sink = jnp.where(top, sink_ref[layer, 4 * j + var],
                                 sink_ref[layer, 4 * j + 2 + var])[:, 0:1]
                m = jnp.maximum(jnp.max(sc, axis=-1, keepdims=True), sink)
                p = jnp.exp(sc - m)
                denom = jnp.sum(p, axis=-1, keepdims=True) + jnp.exp(sink - m)
                p = p * (1.0 / denom)
                zero = jnp.zeros_like(p)
                pw_s[slot, var] = jnp.concatenate(
                    [jnp.where(from_prev, p, zero), jnp.where(from_prev, zero, p)],
                    axis=1).astype(BF16)

        @task(f"pv{b}_{j}", deps=(f"sm{b}_{j}",), m=128, v=20)
        def _():
            o = None
            for var in range(2):
                vcat = v_s[2 * j + var, r0:r0 + 2 * ATT_BLOCK, :]
                ov = _dot(pw_s[slot, var], vcat)
                o = ov if o is None else o + ov
            ao_s[r0:r0 + ATT_BLOCK, (2 * j) * LANES:(2 * j + 1) * LANES] = (
                o[0:ATT_BLOCK].astype(BF16))
            ao_s[r0:r0 + ATT_BLOCK, (2 * j + 1) * LANES:(2 * j + 2) * LANES] = (
                o[ATT_BLOCK:].astype(BF16))

    units = [(b, j) for j in range(N_KV_HEADS) for b in range(nblk)]
    for j in range(N_KV_HEADS):
        make_q(j)
    for b, j in units:
        make_attn(b, j)
    all_att = tuple(f"pv{b}_{j}" for b, j in units)

    def make_proj(name, dst, row0, col0, c):
        @task(f"{name}{c}", m=256)
        def _():
            dst[row0:row0 + T, cols(c)] = proj(col0 + c * CW)

    def make_gate(name, dst, idx, c):
        @task(f"{name}{c}", m=256, v=60)
        def _():
            dst[:, cols(c)] = jax.nn.sigmoid(proj(gate0 + idx * D + c * CW))

    for c in range(nchunk):
        make_proj("xr", xb_s, CONV_CARRY, base, c)
        make_proj("yr", y_s, 0, base + D, c)
        make_proj("pp", pb_s, P, base + 2 * D, c)
        make_gate("g0_", g0_s, 0, c)
        make_gate("g1_", g1_s, 1, c)
        make_gate("g2_", g2_s, 2, c)

    def make_conv(c):
        @task(f"conv{c}", deps=(f"xr{c}",), v=330)
        def _():
            sl = cols(c)
            xall = xb_s[:, sl]
            xc = convb_ref[:, sl]
            for tap in range(CONV_WIDTH):
                lag = CONV_WIDTH - 1 - tap
                xlag = pltpu.roll(xall, lag, 0) if lag else xall
                xc = xc + xlag[CONV_CARRY:, :] * convw_ref[tap:tap + 1, sl]
            xc_s[:, sl] = xc
            xb_s[0:CONV_CARRY, sl] = xb_s[T:T + CONV_CARRY, sl]

        @task(f"gm{c}", deps=(f"conv{c}",), m=64, v=16)
        def _():
            for lc in range(c * CW // LANES, (c + 1) * CW // LANES):
                ls = slice(lc * LANES, (lc + 1) * LANES)
                res = _dot(xc_s[:, ls].astype(BF16), wg_ref[lc])
                ga_s[:, ls] = res[:, 0:LANES]
                gi_s[:, ls] = res[:, LANES:]

    def make_pool(g):
        sl = cols(g)
        w = POOL_WINDOWS[g]

        @task(f"pool{g}", deps=(f"pp{g}",), v=180)
        def _():
            src, dst, lvl = pb_s, pa_s, 1
            while 2 * lvl < w:
                r = 8 * ((lvl).bit_length())
                dst[r:P + T, sl] = src[r:P + T, sl] + src[r - lvl:P + T - lvl, sl]
                src, dst = dst, (pc_s if dst is pa_s else pa_s)
                lvl *= 2
            ssum = src[P:P + T, sl] + src[P - lvl:P + T - lvl, sl]
            tpos = (si * T + lax.broadcasted_iota(jnp.int32, (T, 1), 0) + 1).astype(F32)
            inv_cnt = 1.0 / jnp.minimum(tpos, float(w))
            pd_s[:, sl] = (ssum * inv_cnt - pb_s[P:P + T, sl]).astype(BF16)
            pb_s[0:P, sl] = pb_s[T:T + P, sl]

        @task(f"pm{g}", deps=(f"pool{g}",), m=64, v=16)
        def _():
            mixed = _dot(pd_s[:, sl], wpg_ref[g])
            po_s[:, sl] = (mixed * psc_ref[:, sl]).astype(BF16)

    for c in range(nchunk):
        make_conv(c)
        make_pool(c)
    all_pool = tuple(f"pm{g}" for g in range(nchunk))

    lam = lam_ref[...]
    ls8 = LRU_C * (jnp.minimum(lam, 0.0) - jnp.log1p(jnp.exp(-jnp.abs(lam))))
    rowi = lax.broadcasted_iota(jnp.int32, (SUBLANES, CW), 0)
    hcar = {}
    nstep = T // (2 * SUBLANES)

    def scan_rows(r0, sl, hprev):
        rows = slice(r0, r0 + SUBLANES)
        r = jax.nn.sigmoid(ga_s[rows, sl] + ba_ref[:, sl])
        i = jax.nn.sigmoid(gi_s[rows, sl] + bi_ref[:, sl])
        log_a = r * ls8[:, sl]
        a = jnp.exp(log_a)
        y = jnp.tanh(-log_a) * (a * a + 1.0)
        mult = jnp.where(y > 0.0, y * lax.rsqrt(y), 0.0)
        uu = mult * (i * xc_s[rows, sl])
        for kk in (1, 2, 4):
            keep = rowi >= kk
            a_sh = jnp.where(keep, pltpu.roll(a, kk, 0), 1.0)
            u_sh = jnp.where(keep, pltpu.roll(uu, kk, 0), 0.0)
            uu = a * u_sh + uu
            a = a * a_sh
        hrows = a * hprev + uu
        hlast = jnp.broadcast_to(hrows[SUBLANES - 1:SUBLANES, :], (SUBLANES, CW))
        return hrows * jax.nn.gelu(y_s[rows, sl]), hlast

    def make_scan(c, n):
        deps = (f"gm{c}", f"yr{c}") if n == 0 else (f"scan{c}_{n - 1}",)

        @task(f"scan{c}_{n}", deps=deps, v=70)
        def _():
            sl = cols(c)
            r0 = n * 2 * SUBLANES
            h = hst_s[:, sl] if n == 0 else hcar[c]
            if "epilogue" in anchor:
                h = _after_all(h, anchor.pop("epilogue"))
            out0, h = scan_rows(r0, sl, h)
            out1, h = scan_rows(r0 + SUBLANES, sl, h)
            ro_s[r0:r0 + 2 * SUBLANES, sl] = jnp.concatenate([out0, out1], axis=0).astype(BF16)
            if n == nstep - 1:
                hst_s[:, sl] = h
            else:
                hcar[c] = h

    for c in range(nchunk):
        for n in range(nstep):
            make_scan(c, n)
    all_scan = tuple(f"scan{c}_{nstep - 1}" for c in range(nchunk))

    def make_branch(name, dst, gate, gname, src, w_ref, deps, c):
        @task(f"{name}{c}", deps=deps + (f"{gname}{c}",), m=256, v=16)
        def _():
            dst[:, cols(c)] = gate[:, cols(c)] * _dot(src[...], w_ref[:, cols(c)])

    def make_merge(c):
        @task(f"merge{c}", deps=all_scan + (f"g1_{c}", f"abr{c}", f"pbr{c}"), m=256, v=50)
        def _():
            sl = cols(c)
            rnn = g1_s[:, sl] * _dot(ro_s[...], wrnn_ref[:, sl])
            merged = (acc_s[:, sl] + rnn) + p2_s[:, sl]
            if c == nchunk - 1:
                merged = _after_all(merged, anchor.pop("prenorm_next"))
            mg_s[:, sl] = merged.astype(BF16)

    for c in range(nchunk):
        make_branch("abr", acc_s, g0_s, "g0_", ao_s, wattn_ref, all_att, c)
        make_branch("pbr", p2_s, g2_s, "g2_", po_s, wpool_ref, all_pool, c)
        make_merge(c)

    MXU_GAP = 640
    fillers = [f"xr{0}"]
    for c in range(1, nchunk):
        fillers += [f"q{c}", f"xr{c}", f"yr{c - 1}"]
    fillers += [f"yr{nchunk - 1}"] + [f"pp{c}" for c in range(nchunk)]
    fillers += [f"g{i}_{c}" for i in (0, 2, 1) for c in range(nchunk)]
    lead = 2
    main = [f"wo{0}", "kv", f"wo{1}", "q0"] + [f"wo{c}" for c in range(2, nchunk)] + fillers[:lead]
    rest = fillers[lead:]
    for n, (b, j) in enumerate(units):
        main.append(f"qk{b}_{j}")
        main += rest[n * len(rest) // len(units):(n + 1) * len(rest) // len(units)]
    main += [f"abr{c}" for c in range(nchunk)] + [f"pbr{c}" for c in range(nchunk)]
    n_before_merge = len(main)
    main += [f"merge{c}" for c in range(nchunk)]
    follow = {f"qk{b}_{j}": (f"sm{b}_{j}", f"pv{b}_{j}") for b, j in units}
    follow.update({f"xr{c}": (f"conv{c}", f"gm{c}") for c in range(nchunk)})
    follow.update({f"abr{c}": (f"pool{c}", f"pm{c}") for c in range(nchunk)})
    scan_next = [0] * nchunk
    done = []
    spent = [0]
    deferred = []

    def emit(name):
        if name in done:
            return
        fn, deps, m, v = tasks[name]
        for d in deps:
            emit(d)
        fn()
        done.append(name)
        spent[0] += m

    def scans_left():
        return sum(nstep - s for s in scan_next)

    def make_wout(c):
        @task(f"wo{c}", m=256)
        def _():
            mix_s[:, cols(c)] = _dot(mg_s[...], wout_ref[:, cols(c)])

    for c in range(nchunk):
        make_wout(c)

    @task("epilogue", deps=tuple(f"wo{c}" for c in range(nchunk)), v=250)
    def _():
        out = hkeep_s[...] + _rms(mix_s[...], gpost_ref[...])
        o_ref[...] = out
        anchor["epilogue"] = out

    @task("prenorm_next", v=200)
    def _():
        u_next = _rms(hnext_ref[...], gpre_ref[...])
        u_s[...] = u_next.astype(BF16)
        anchor["prenorm_next"] = u_next

    beside = {f"wo{nchunk - 1}": "epilogue", f"merge{0}": "prenorm_next"}
    for pos, name in enumerate(main):
        emit(name)
        if name in beside:
            emit(beside[name])
        for item in list(deferred):
            if spent[0] >= item[0]:
                emit(item[1])
                deferred.remove(item)
        if name in follow:
            producer, consumer = follow[name]
            emit(producer)
            deferred.append((spent[0] + MXU_GAP, consumer))
        quota = -(-scans_left() // max(1, n_before_merge - pos))
        for c in sorted(range(nchunk), key=lambda c: scan_next[c]):
            if quota > 0 and scan_next[c] < nstep and f"gm{c}" in done and f"yr{c}" in done:
                emit(f"scan{c}_{scan_next[c]}")
                scan_next[c] += 1
                quota -= 1
    assert set(done) == set(tasks), set(tasks) ^ set(done)

    k_s[:, 0:ATT_BLOCK, :] = k_s[:, T:T + ATT_BLOCK, :]
    v_s[:, 0:ATT_BLOCK, :] = v_s[:, T:T + ATT_BLOCK, :]
    hkeep_s[...] = h_ref[...]


def _mlp_kernel(h_ref, gpre_ref, gpost_ref, wup_ref, wdown_ref, o_ref, m_s, acc_s):
    d_ff = wup_ref.shape[1]
    m_s[...] = _rms(h_ref[...], gpre_ref[...]).astype(BF16)
    for c in range(d_ff // FF_CHUNK):
        sl = slice(c * FF_CHUNK, (c + 1) * FF_CHUNK)
        up = jnp.maximum(_dot(m_s[...], wup_ref[:, sl]), 0.0)
        part = _dot((up * up).astype(BF16), wdown_ref[sl, :])
        if c == 0:
            acc_s[...] = part
        else:
            acc_s[...] = acc_s[...] + part
    o_ref[...] = h_ref[...] + _rms(acc_s[...], gpost_ref[...])


def _layer_spec(stacked, layer):
    nd = stacked.ndim - 1
    return pl.BlockSpec((None,) + stacked.shape[1:], lambda *_: (layer,) + (0,) * nd,
                        pipeline_mode=pl.Buffered(1))


def _mixer_call(layer, h, tab, gpre, gpost, win, sinks, wattn, convw, convb, wg, ba, bi, lam,
                wrnn, wpg, psc, wpool, wout):
    B, S, D = h.shape
    T = MIX_TILE
    ns = S // T
    ntile = B * ns

    def tile_at(shift):
        def index_map(g):
            t = jnp.clip(g + shift, 0, ntile - 1)
            return (t // ns, t % ns, 0)
        return pl.BlockSpec((None, T, D), index_map)

    in_specs = [
        tile_at(0), tile_at(1),
        pl.BlockSpec((3, T, LANES), lambda g: (0, jnp.minimum(g, ntile - 1) % ns, 0)),
        _layer_spec(gpre, layer), _layer_spec(gpost, layer), _layer_spec(win, layer),
        pl.BlockSpec(memory_space=pltpu.SMEM),
        _layer_spec(wattn, layer), _layer_spec(convw, layer), _layer_spec(convb, layer),
        _layer_spec(wg, layer), _layer_spec(ba, layer), _layer_spec(bi, layer),
        _layer_spec(lam, layer), _layer_spec(wrnn, layer), _layer_spec(wpg, layer),
        _layer_spec(psc, layer), _layer_spec(wpool, layer), _layer_spec(wout, layer),
    ]
    nvar = 2 * N_KV_HEADS
    scratch = [
        pltpu.VMEM((T, D), BF16),
        pltpu.VMEM((T, D), BF16),
        pltpu.VMEM((nvar, ATT_BLOCK + T, LANES), BF16),
        pltpu.VMEM((nvar, ATT_BLOCK + T, LANES), BF16),
        pltpu.VMEM((T, D), BF16),
        pltpu.VMEM((CONV_CARRY + T, D), F32),
        pltpu.VMEM((T, D), F32),
        pltpu.VMEM((T, D), F32),
        pltpu.VMEM((T, D), F32),
        pltpu.VMEM((T, D), F32),
        pltpu.VMEM((SUBLANES, D), F32),
        pltpu.VMEM((T, D), BF16),
        pltpu.VMEM((POOL_CARRY + T, D), F32),
        pltpu.VMEM((POOL_CARRY + T, D), F32),
        pltpu.VMEM((POOL_CARRY + T, D), F32),
        pltpu.VMEM((T, D), BF16),
        pltpu.VMEM((T, D), F32),
        pltpu.VMEM((T, D), F32),
        pltpu.VMEM((T, D), F32),
        pltpu.VMEM((T, D), F32),
        pltpu.VMEM((T, D), F32),
        pltpu.VMEM((T, D), BF16),
        pltpu.VMEM((2, 2, 2 * ATT_BLOCK, ATT_BLOCK), F32),
        pltpu.VMEM((2, 2, 2 * ATT_BLOCK, 2 * ATT_BLOCK), BF16),
        pltpu.VMEM((T, D), BF16),
        pltpu.VMEM((T, D), F32),
        pltpu.VMEM((T, D), F32),
    ]
    return pl.pallas_call(
        functools.partial(_mixer_kernel, ns, layer),
        grid=(ntile + 1,),
        in_specs=in_specs,
        out_specs=tile_at(-1),
        out_shape=jax.ShapeDtypeStruct(h.shape, h.dtype),
        scratch_shapes=scratch,
        compiler_params=pltpu.CompilerParams(
            dimension_semantics=("arbitrary",),
            vmem_limit_bytes=VMEM_LIMIT_BYTES),
        name="mixer",
    )(h, h, tab, gpre, gpost, win, sinks, wattn, convw, convb, wg, ba, bi, lam, wrnn,
      wpg, psc, wpool, wout)


def _mlp_call(layer, h, gpre, gpost, wup, wdown):
    B, S, D = h.shape
    T = MLP_TILE
    tile = pl.BlockSpec((None, T, D), lambda b, s: (b, s, 0))
    return pl.pallas_call(
        _mlp_kernel,
        grid=(B, S // T),
        in_specs=[tile, _layer_spec(gpre, layer), _layer_spec(gpost, layer),
                  _layer_spec(wup, layer), _layer_spec(wdown, layer)],
        out_specs=tile,
        out_shape=jax.ShapeDtypeStruct(h.shape, h.dtype),
        scratch_shapes=[pltpu.VMEM((T, D), BF16), pltpu.VMEM((T, D), F32)],
        compiler_params=pltpu.CompilerParams(
            dimension_semantics=("arbitrary", "arbitrary"),
            vmem_limit_bytes=VMEM_LIMIT_BYTES),
        name="mlp",
    )(h, gpre, gpost, wup, wdown)


def _rope_tables(seq):
    inv_freq = ROPE_THETA ** (-jnp.arange(0, ROT_DIM, 2, dtype=F32) / ROT_DIM)
    ang = jnp.arange(seq, dtype=F32)[:, None] * inv_freq[None, :]
    cos, sin = jnp.cos(ang), jnp.sin(ang)
    pad = jnp.zeros((seq, HEAD_DIM - ROT_DIM), F32)
    c = jnp.concatenate([cos, cos, pad + 1.0], axis=1)
    s1 = jnp.concatenate([-sin, jnp.zeros_like(sin), pad], axis=1)
    s2 = jnp.concatenate([jnp.zeros_like(sin), sin, pad], axis=1)
    tab = jnp.stack([c, s1, s2])
    tab = jnp.concatenate([tab] * (LANES // HEAD_DIM), axis=2)
    return tab * (float(HEAD_DIM) ** -0.5)


def _gate_weights(w_a, w_i):
    L, nb, bw, _ = w_a.shape

    def bd(w):
        w = w.reshape(L, nb // 2, 2, bw, bw)
        z = jnp.zeros_like(w[:, :, 0])
        top = jnp.concatenate([w[:, :, 0], z], axis=-1)
        bot = jnp.concatenate([z, w[:, :, 1]], axis=-1)
        return jnp.concatenate([top, bot], axis=-2)

    return jnp.concatenate([bd(w_a), bd(w_i)], axis=-1).astype(BF16)


def kernel(x, norm_mix_pre, norm_mix_post, w_in, attn_sinks, w_attn_br, conv_w, conv_b,
           w_rg_a, b_rg_a, w_rg_i, b_rg_i, lru_lambda, w_rnn_br, w_pool_groups, pool_scale,
           w_pool_br, w_out, norm_mlp_pre, norm_mlp_post, w_mlp_up, w_mlp_down):
    B, S, D = x.shape
    depth = w_in.shape[0]
    assert S % MIX_TILE == 0 and S % MLP_TILE == 0 and MIX_TILE % ATT_BLOCK == 0
    tab = _rope_tables(S)
    rows = lambda a: a[:, None, :]
    bf = lambda a: a.astype(BF16)
    mixer_params = (
        rows(norm_mix_pre), rows(norm_mix_post), bf(w_in), attn_sinks, bf(w_attn_br), conv_w,
        rows(conv_b), _gate_weights(w_rg_a, w_rg_i), rows(b_rg_a), rows(b_rg_i),
        rows(lru_lambda), bf(w_rnn_br), bf(w_pool_groups), rows(pool_scale), bf(w_pool_br),
        bf(w_out))
    mlp_params = (rows(norm_mlp_pre), rows(norm_mlp_post), bf(w_mlp_up), bf(w_mlp_down))
    h = x
    for l in range(depth):
        h = _mixer_call(l, h, tab, *mixer_params)
        h = _mlp_call(l, h, *mlp_params)
    return h
```

```python
import functools

import jax
import jax.numpy as jnp
from jax import lax
from jax.experimental import pallas as pl
from jax.experimental.pallas import tpu as pltpu

F32 = jnp.float32
BF16 = jnp.bfloat16

HEAD_DIM = 64
N_KV_HEADS = 4
ATT_BLOCK = 128
ROT_DIM = HEAD_DIM // 4
ROPE_THETA = 500000.0
CONV_WIDTH = 4
LRU_C = 8.0
POOL_WINDOWS = (2, 4, 8, 16)
EPS = 1e-6
NEG_INF = -1e30

LANES = 128
SUBLANES = 8
CONV_CARRY = SUBLANES
POOL_CARRY = 32
VMEM_LIMIT_BYTES = 58 * 1024 * 1024

MIX_TILE = 256
MLP_TILE = 1024
FF_CHUNK = 1024


def _dot(a, b):
    return jnp.dot(a, b, preferred_element_type=F32)


def _dot_nt(a, b):
    return lax.dot_general(a, b, (((1,), (1,)), ((), ())), preferred_element_type=F32)


def _rms(x, g):
    ms = jnp.mean(x * x, axis=-1, keepdims=True)
    return (x * lax.rsqrt(ms + EPS)) * g


def _rope(x, c, s1, s2):
    half = ROT_DIM // 2
    return x * c + pltpu.roll(x, LANES - half, 1) * s1 + pltpu.roll(x, half, 1) * s2


def _after(x, probe):
    sh = jnp.uint32(16)
    z = lax.shift_right_logical(lax.shift_right_logical(pltpu.bitcast(probe, jnp.uint32), sh), sh)
    return pltpu.bitcast(pltpu.bitcast(x, jnp.uint32) | z, F32)


def _after_all(x, y):
    acc = y[0:SUBLANES, :]
    for r in range(SUBLANES, y.shape[0], SUBLANES):
        acc = acc + y[r:r + SUBLANES, :]
    blk = acc[:, 0:LANES]
    for c in range(LANES, y.shape[1], LANES):
        blk = blk + acc[:, c:c + LANES]
    top = _after(x[0:SUBLANES, 0:LANES], blk)
    if x.shape[1] > LANES:
        top = jnp.concatenate([top, x[0:SUBLANES, LANES:]], axis=1)
    if x.shape[0] > SUBLANES:
        top = jnp.concatenate([top, x[SUBLANES:, :]], axis=0)
    return top


def _head_variants(x, kc):
    lane = lax.broadcasted_iota(jnp.int32, x.shape, 1)
    lo = lane < HEAD_DIM
    xs = pltpu.roll(x, HEAD_DIM, 1)
    zero = jnp.zeros_like(x)
    return {
        (2 * kc, 0): jnp.where(lo, x, zero),
        (2 * kc, 1): jnp.where(lo, zero, xs),
        (2 * kc + 1, 0): jnp.where(lo, xs, zero),
        (2 * kc + 1, 1): jnp.where(lo, zero, x),
    }


def _mixer_kernel(seq_tiles, layer, h_ref, hprev_ref, hnext_ref, tab_ref, gpre_ref, gpost_ref,
                  win_ref, sink_ref, wattn_ref,
                  convw_ref, convb_ref, wg_ref, ba_ref, bi_ref, lam_ref, wrnn_ref,
                  wpg_ref, psc_ref, wpool_ref, wout_ref, o_ref,
                  u_s, q_s, k_s, v_s, ao_s, xb_s, xc_s, ga_s, gi_s, y_s, hst_s, ro_s,
                  pb_s, pa_s, pc_s, po_s, acc_s, g0_s, g1_s, g2_s, p2_s, mg_s,
                  sc_s, pw_s, pd_s, mix_s):
    g = pl.program_id(0)
    si = lax.rem(g, seq_tiles)
    T, D = h_ref.shape
    nblk = T // ATT_BLOCK

    @pl.when(g == 0)
    def _():
        u_s[...] = _rms(h_ref[...], gpre_ref[...]).astype(BF16)
        mg_s[...] = jnp.zeros(mg_s.shape, BF16)

    @pl.when(si == 0)
    def _():
        k_s[:, 0:ATT_BLOCK, :] = jnp.zeros((2 * N_KV_HEADS, ATT_BLOCK, LANES), BF16)
        v_s[:, 0:ATT_BLOCK, :] = jnp.zeros((2 * N_KV_HEADS, ATT_BLOCK, LANES), BF16)
        xb_s[0:CONV_CARRY, :] = jnp.zeros((CONV_CARRY, D), F32)
        pb_s[0:POOL_CARRY, :] = jnp.zeros((POOL_CARRY, D), F32)
        hst_s[...] = jnp.zeros(hst_s.shape, F32)

    kvw = N_KV_HEADS * HEAD_DIM
    base = D + 2 * kvw
    gate0 = base + 3 * D
    P = POOL_CARRY
    CW = D // len(POOL_WINDOWS)
    nchunk = D // CW
    tasks = {}
    anchor = {}

    def task(name, deps=(), m=0, v=0):
        def reg(fn):
            tasks[name] = (fn, tuple(deps), m, v)
            return fn
        return reg

    def cols(c):
        return slice(c * CW, (c + 1) * CW)

    def proj(col0, width=CW):
        return _dot(u_s[...], win_ref[:, col0:col0 + width])

    inv_scale = float(HEAD_DIM) ** 0.5
    cur = slice(ATT_BLOCK, ATT_BLOCK + T)

    @task("kv", m=128, v=150)
    def _():
        c_k, s1_k, s2_k = (tab_ref[i] * inv_scale for i in range(3))
        k = proj(D, kvw)
        v = proj(D + kvw, kvw)
        for kc in range(kvw // LANES):
            sl = slice(kc * LANES, (kc + 1) * LANES)
            for (head, var), val in _head_variants(_rope(k[:, sl], c_k, s1_k, s2_k), kc).items():
                k_s[2 * head + var, cur, :] = val.astype(BF16)
            for (head, var), val in _head_variants(v[:, sl], kc).items():
                v_s[2 * head + var, cur, :] = val.astype(BF16)

    def make_q(j):
        @task(f"q{j}", m=256, v=100)
        def _():
            q = proj(2 * j * LANES, 2 * LANES)
            for c in range(2):
                sl = slice((2 * j + c) * LANES, (2 * j + c + 1) * LANES)
                q_s[:, sl] = _rope(q[:, c * LANES:(c + 1) * LANES],
                                   tab_ref[0], tab_ref[1], tab_ref[2]).astype(BF16)

    def att_masks():
        row = lax.broadcasted_iota(jnp.int32, (2 * ATT_BLOCK, ATT_BLOCK), 0)
        col = lax.broadcasted_iota(jnp.int32, (2 * ATT_BLOCK, ATT_BLOCK), 1)
        qpos = jnp.where(row >= ATT_BLOCK, row - ATT_BLOCK, row)
        return col > qpos, row < ATT_BLOCK

    def make_attn(b, j):
        r0 = b * ATT_BLOCK
        slot = (j * nblk + b) % 2

        @task(f"qk{b}_{j}", deps=("kv", f"q{j}"), m=128, v=40)
        def _():
            from_prev, _ = att_masks()
            qst = jnp.concatenate(
                [q_s[r0:r0 + ATT_BLOCK, (2 * j) * LANES:(2 * j + 1) * LANES],
                 q_s[r0:r0 + ATT_BLOCK, (2 * j + 1) * LANES:(2 * j + 2) * LANES]], axis=0)
            for var in range(2):
                kcat = k_s[2 * j + var, r0:r0 + 2 * ATT_BLOCK, :]
                s = _dot_nt(qst, kcat)
                s_prev = s[:, 0:ATT_BLOCK]
                if b == 0:
                    s_prev = jnp.where(si > 0, s_prev, NEG_INF)
                sc_s[slot, var] = jnp.where(from_prev, s_prev, s[:, ATT_BLOCK:])

        @task(f"sm{b}_{j}", deps=(f"qk{b}_{j}",), v=220)
        def _():
            from_prev, top = att_masks()
            for var in range(2):
                sc = sc_s[slot, var]
                sink = jnp.where(top, sink_ref[layer, 4 * j + var],
                                 sink_ref[layer, 4 * j + 2 + var])[:, 0:1]
                m = jnp.maximum(jnp.max(sc, axis=-1, keepdims=True), sink)
                p = jnp.exp(sc - m)
                denom = jnp.sum(p, axis=-1, keepdims=True) + jnp.exp(sink - m)
                p = p * (1.0 / denom)
                zero = jnp.zeros_like(p)
                pw_s[slot, var] = jnp.concatenate(
                    [jnp.where(from_prev, p, zero), jnp.where(from_prev, zero, p)],
                    axis=1).astype(BF16)

        @task(f"pv{b}_{j}", deps=(f"sm{b}_{j}",), m=128, v=20)
        def _():
            o = None
            for var in range(2):
                vcat = v_s[2 * j + var, r0:r0 + 2 * ATT_BLOCK, :]
                ov = _dot(pw_s[slot, var], vcat)
                o = ov if o is None else o + ov
            ao_s[r0:r0 + ATT_BLOCK, (2 * j) * LANES:(2 * j + 1) * LANES] = (
                o[0:ATT_BLOCK].astype(BF16))
            ao_s[r0:r0 + ATT_BLOCK, (2 * j + 1) * LANES:(2 * j + 2) * LANES] = (
                o[ATT_BLOCK:].astype(BF16))

    units = [(b, j) for j in range(N_KV_HEADS) for b in range(nblk)]
    for j in range(N_KV_HEADS):
        make_q(j)
    for b, j in units:
        make_attn(b, j)
    all_att = tuple(f"pv{b}_{j}" for b, j in units)

    def make_proj(name, dst, row0, col0, c):
        @task(f"{name}{c}", m=256)
        def _():
            dst[row0:row0 + T, cols(c)] = proj(col0 + c * CW)

    def make_gate(name, dst, idx, c):
        @task(f"{name}{c}", m=256, v=60)
        def _():
            dst[:, cols(c)] = jax.nn.sigmoid(proj(gate0 + idx * D + c * CW))

    for c in range(nchunk):
        make_proj("xr", xb_s, CONV_CARRY, base, c)
        make_proj("yr", y_s, 0, base + D, c)
        make_proj("pp", pb_s, P, base + 2 * D, c)
        make_gate("g0_", g0_s, 0, c)
        make_gate("g1_", g1_s, 1, c)
        make_gate("g2_", g2_s, 2, c)

    def make_conv(c):
        @task(f"conv{c}", deps=(f"xr{c}",), v=330)
        def _():
            sl = cols(c)
            xall = xb_s[:, sl]
            xc = convb_ref[:, sl]
            for tap in range(CONV_WIDTH):
                lag = CONV_WIDTH - 1 - tap
                xlag = pltpu.roll(xall, lag, 0) if lag else xall
                xc = xc + xlag[CONV_CARRY:, :] * convw_ref[tap:tap + 1, sl]
            xc_s[:, sl] = xc
            xb_s[0:CONV_CARRY, sl] = xb_s[T:T + CONV_CARRY, sl]

        @task(f"gm{c}", deps=(f"conv{c}",), m=64, v=16)
        def _():
            for lc in range(c * CW // LANES, (c + 1) * CW // LANES):
                ls = slice(lc * LANES, (lc + 1) * LANES)
                res = _dot(xc_s[:, ls].astype(BF16), wg_ref[lc])
                ga_s[:, ls] = res[:, 0:LANES]
                gi_s[:, ls] = res[:, LANES:]

    def make_pool(g):
        sl = cols(g)
        w = POOL_WINDOWS[g]

        @task(f"pool{g}", deps=(f"pp{g}",), v=180)
        def _():
            src, dst, lvl = pb_s, pa_s, 1
            while 2 * lvl < w:
                r = 8 * ((lvl).bit_length())
                dst[r:P + T, sl] = src[r:P + T, sl] + src[r - lvl:P + T - lvl, sl]
                src, dst = dst, (pc_s if dst is pa_s else pa_s)
                lvl *= 2
            ssum = src[P:P + T, sl] + src[P - lvl:P + T - lvl, sl]
            tpos = (si * T + lax.broadcasted_iota(jnp.int32, (T, 1), 0) + 1).astype(F32)
            inv_cnt = 1.0 / jnp.minimum(tpos, float(w))
            pd_s[:, sl] = (ssum * inv_cnt - pb_s[P:P + T, sl]).astype(BF16)
            pb_s[0:P, sl] = pb_s[T:T + P, sl]

        @task(f"pm{g}", deps=(f"pool{g}",), m=64, v=16)
        def _():
            mixed = _dot(pd_s[:, sl], wpg_ref[g])
            po_s[:, sl] = (mixed * psc_ref[:, sl]).astype(BF16)

    for c in range(nchunk):
        make_conv(c)
        make_pool(c)
    all_pool = tuple(f"pm{g}" for g in range(nchunk))

    lam = lam_ref[...]
    ls8 = LRU_C * (jnp.minimum(lam, 0.0) - jnp.log1p(jnp.exp(-jnp.abs(lam))))
    hcar = {}
    nstep = T // (2 * SUBLANES)

    def scan_rows(r0, sl, hprev):
        rows = slice(r0, r0 + SUBLANES)
        rowi = lax.broadcasted_iota(jnp.int32, (SUBLANES, CW), 0)
        r = jax.nn.sigmoid(ga_s[rows, sl] + ba_ref[:, sl])
        i = jax.nn.sigmoid(gi_s[rows, sl] + bi_ref[:, sl])
        log_a = r * ls8[:, sl]
        a = jnp.exp(log_a)
        y = jnp.tanh(-log_a) * (a * a + 1.0)
        mult = jnp.where(y > 0.0, y * lax.rsqrt(y), 0.0)
        uu = mult * (i * xc_s[rows, sl])
        for kk in (1, 2, 4):
            keep = rowi >= kk
            a_sh = jnp.where(keep, pltpu.roll(a, kk, 0), 1.0)
            u_sh = jnp.where(keep, pltpu.roll(uu, kk, 0), 0.0)
            uu = a * u_sh + uu
            a = a * a_sh
        hrows = a * hprev + uu
        hlast = jnp.broadcast_to(hrows[SUBLANES - 1:SUBLANES, :], (SUBLANES, CW))
        return hrows * jax.nn.gelu(y_s[rows, sl]), hlast

    def make_scan(c, n):
        deps = (f"gm{c}", f"yr{c}") if n == 0 else (f"scan{c}_{n - 1}",)

        @task(f"scan{c}_{n}", deps=deps, v=70)
        def _():
            sl = cols(c)
            r0 = n * 2 * SUBLANES
            h = hst_s[:, sl] if n == 0 else hcar[c]
            if "epilogue" in anchor:
                h = _after_all(h, anchor.pop("epilogue"))
            out0, h = scan_rows(r0, sl, h)
            out1, h = scan_rows(r0 + SUBLANES, sl, h)
            ro_s[r0:r0 + 2 * SUBLANES, sl] = jnp.concatenate([out0, out1], axis=0).astype(BF16)
            if n == nstep - 1:
                hst_s[:, sl] = h
            else:
                hcar[c] = h

    for c in range(nchunk):
        for n in range(nstep):
            make_scan(c, n)
    all_scan = tuple(f"scan{c}_{nstep - 1}" for c in range(nchunk))

    def make_branch(name, dst, gate, gname, src, w_ref, deps, c):
        @task(f"{name}{c}", deps=deps + (f"{gname}{c}",), m=256, v=16)
        def _():
            dst[:, cols(c)] = gate[:, cols(c)] * _dot(src[...], w_ref[:, cols(c)])

    def make_merge(c):
        @task(f"merge{c}", deps=all_scan + (f"g1_{c}", f"abr{c}", f"pbr{c}"), m=256, v=50)
        def _():
            sl = cols(c)
            rnn = g1_s[:, sl] * _dot(ro_s[...], wrnn_ref[:, sl])
            merged = (acc_s[:, sl] + rnn) + p2_s[:, sl]
            if c == nchunk - 1:
                merged = _after_all(merged, anchor.pop("prenorm_next"))
            mg_s[:, sl] = merged.astype(BF16)

    for c in range(nchunk):
        make_branch("abr", acc_s, g0_s, "g0_", ao_s, wattn_ref, all_att, c)
        make_branch("pbr", p2_s, g2_s, "g2_", po_s, wpool_ref, all_pool, c)
        make_merge(c)

    MXU_GAP = 640
    fillers = [f"xr{0}"]
    for c in range(1, nchunk):
        fillers += [f"q{c}", f"xr{c}", f"yr{c - 1}"]
    fillers += [f"yr{nchunk - 1}"] + [f"pp{c}" for c in range(nchunk)]
    fillers += [f"g{i}_{c}" for i in (0, 2, 1) for c in range(nchunk)]
    lead = 2
    main = [f"wo{0}", "kv", f"wo{1}", "q0"] + [f"wo{c}" for c in range(2, nchunk)] + fillers[:lead]
    rest = fillers[lead:]
    for n, (b, j) in enumerate(units):
        main.append(f"qk{b}_{j}")
        main += rest[n * len(rest) // len(units):(n + 1) * len(rest) // len(units)]
    main += [f"abr{c}" for c in range(nchunk)] + [f"pbr{c}" for c in range(nchunk)]
    n_before_merge = len(main)
    main += [f"merge{c}" for c in range(nchunk)]
    follow = {f"qk{b}_{j}": (f"sm{b}_{j}", f"pv{b}_{j}") for b, j in units}
    follow.update({f"xr{c}": (f"conv{c}", f"gm{c}") for c in range(nchunk)})
    follow.update({f"abr{c}": (f"pool{c}", f"pm{c}") for c in range(nchunk)})
    scan_next = [0] * nchunk
    done = []
    spent = [0]
    deferred = []

    def emit(name):
        if name in done:
            return
        fn, deps, m, v = tasks[name]
        for d in deps:
            emit(d)
        fn()
        done.append(name)
        spent[0] += m

    def scans_left():
        return sum(nstep - s for s in scan_next)

    def make_wout(c):
        @task(f"wo{c}", m=256)
        def _():
            mix_s[:, cols(c)] = _dot(mg_s[...], wout_ref[:, cols(c)])

    for c in range(nchunk):
        make_wout(c)

    @task("epilogue", deps=tuple(f"wo{c}" for c in range(nchunk)), v=250)
    def _():
        out = hprev_ref[...] + _rms(mix_s[...], gpost_ref[...])
        o_ref[...] = out
        anchor["epilogue"] = out

    @task("prenorm_next", v=200)
    def _():
        u_next = _rms(hnext_ref[...], gpre_ref[...])
        u_s[...] = u_next.astype(BF16)
        anchor["prenorm_next"] = u_next

    beside = {f"wo{nchunk - 1}": "epilogue", f"merge{0}": "prenorm_next"}
    for pos, name in enumerate(main):
        emit(name)
        if name in beside:
            emit(beside[name])
        for item in list(deferred):
            if spent[0] >= item[0]:
                emit(item[1])
                deferred.remove(item)
        if name in follow:
            producer, consumer = follow[name]
            emit(producer)
            deferred.append((spent[0] + MXU_GAP, consumer))
        quota = -(-scans_left() // max(1, n_before_merge - pos))
        for c in sorted(range(nchunk), key=lambda c: scan_next[c]):
            if quota > 0 and scan_next[c] < nstep and f"gm{c}" in done and f"yr{c}" in done:
                emit(f"scan{c}_{scan_next[c]}")
                scan_next[c] += 1
                quota -= 1
    assert set(done) == set(tasks), set(tasks) ^ set(done)

    k_s[:, 0:ATT_BLOCK, :] = k_s[:, T:T + ATT_BLOCK, :]
    v_s[:, 0:ATT_BLOCK, :] = v_s[:, T:T + ATT_BLOCK, :]


def _mlp_kernel(h_ref, gpre_ref, gpost_ref, wup_ref, wdown_ref, o_ref, m_s, acc_s):
    d_ff = wup_ref.shape[1]
    m_s[...] = _rms(h_ref[...], gpre_ref[...]).astype(BF16)
    for c in range(d_ff // FF_CHUNK):
        sl = slice(c * FF_CHUNK, (c + 1) * FF_CHUNK)
        up = jnp.maximum(_dot(m_s[...], wup_ref[:, sl]), 0.0)
        part = _dot((up * up).astype(BF16), wdown_ref[sl, :])
        if c == 0:
            acc_s[...] = part
        else:
            acc_s[...] = acc_s[...] + part
    o_ref[...] = h_ref[...] + _rms(acc_s[...], gpost_ref[...])


def _layer_spec(stacked, layer):
    nd = stacked.ndim - 1
    return pl.BlockSpec((None,) + stacked.shape[1:], lambda *_: (layer,) + (0,) * nd,
                        pipeline_mode=pl.Buffered(1))


def _mixer_call(layer, h, tab, gpre, gpost, win, sinks, wattn, convw, convb, wg, ba, bi, lam,
                wrnn, wpg, psc, wpool, wout):
    B, S, D = h.shape
    T = MIX_TILE
    ns = S // T
    ntile = B * ns

    def tile_at(shift):
        def index_map(g):
            t = jnp.clip(g + shift, 0, ntile - 1)
            return (t // ns, t % ns, 0)
        return pl.BlockSpec((None, T, D), index_map)

    in_specs = [
        tile_at(0), tile_at(-1), tile_at(1),
        pl.BlockSpec((3, T, LANES), lambda g: (0, jnp.minimum(g, ntile - 1) % ns, 0)),
        _layer_spec(gpre, layer), _layer_spec(gpost, layer), _layer_spec(win, layer),
        pl.BlockSpec(memory_space=pltpu.SMEM),
        _layer_spec(wattn, layer), _layer_spec(convw, layer), _layer_spec(convb, layer),
        _layer_spec(wg, layer), _layer_spec(ba, layer), _layer_spec(bi, layer),
        _layer_spec(lam, layer), _layer_spec(wrnn, layer), _layer_spec(wpg, layer),
        _layer_spec(psc, layer), _layer_spec(wpool, layer), _layer_spec(wout, layer),
    ]
    nvar = 2 * N_KV_HEADS
    scratch = [
        pltpu.VMEM((T, D), BF16),
        pltpu.VMEM((T, D), BF16),
        pltpu.VMEM((nvar, ATT_BLOCK + T, LANES), BF16),
        pltpu.VMEM((nvar, ATT_BLOCK + T, LANES), BF16),
        pltpu.VMEM((T, D), BF16),
        pltpu.VMEM((CONV_CARRY + T, D), F32),
        pltpu.VMEM((T, D), F32),
        pltpu.VMEM((T, D), F32),
        pltpu.VMEM((T, D), F32),
        pltpu.VMEM((T, D), F32),
        pltpu.VMEM((SUBLANES, D), F32),
        pltpu.VMEM((T, D), BF16),
        pltpu.VMEM((POOL_CARRY + T, D), F32),
        pltpu.VMEM((POOL_CARRY + T, D), F32),
        pltpu.VMEM((POOL_CARRY + T, D), F32),
        pltpu.VMEM((T, D), BF16),
        pltpu.VMEM((T, D), F32),
        pltpu.VMEM((T, D), F32),
        pltpu.VMEM((T, D), F32),
        pltpu.VMEM((T, D), F32),
        pltpu.VMEM((T, D), F32),
        pltpu.VMEM((T, D), BF16),
        pltpu.VMEM((2, 2, 2 * ATT_BLOCK, ATT_BLOCK), F32),
        pltpu.VMEM((2, 2, 2 * ATT_BLOCK, 2 * ATT_BLOCK), BF16),
        pltpu.VMEM((T, D), BF16),
        pltpu.VMEM((T, D), F32),
    ]
    return pl.pallas_call(
        functools.partial(_mixer_kernel, ns, layer),
        grid=(ntile + 1,),
        in_specs=in_specs,
        out_specs=tile_at(-1),
        out_shape=jax.ShapeDtypeStruct(h.shape, h.dtype),
        scratch_shapes=scratch,
        compiler_params=pltpu.CompilerParams(
            dimension_semantics=("arbitrary",),
            vmem_limit_bytes=VMEM_LIMIT_BYTES),
        name="mixer",
    )(h, h, h, tab, gpre, gpost, win, sinks, wattn, convw, convb, wg, ba, bi, lam, wrnn,
      wpg, psc, wpool, wout)


def _mlp_call(layer, h, gpre, gpost, wup, wdown):
    B, S, D = h.shape
    T = MLP_TILE
    tile = pl.BlockSpec((None, T, D), lambda b, s: (b, s, 0))
    return pl.pallas_call(
        _mlp_kernel,
        grid=(B, S // T),
        in_specs=[tile, _layer_spec(gpre, layer), _layer_spec(gpost, layer),
                  _layer_spec(wup, layer), _layer_spec(wdown, layer)],
        out_specs=tile,
        out_shape=jax.ShapeDtypeStruct(h.shape, h.dtype),
        scratch_shapes=[pltpu.VMEM((T, D), BF16), pltpu.VMEM((T, D), F32)],
        compiler_params=pltpu.CompilerParams(
            dimension_semantics=("arbitrary", "arbitrary"),
            vmem_limit_bytes=VMEM_LIMIT_BYTES),
        name="mlp",
    )(h, gpre, gpost, wup, wdown)


def _rope_tables(seq):
    inv_freq = ROPE_THETA ** (-jnp.arange(0, ROT_DIM, 2, dtype=F32) / ROT_DIM)
    ang = jnp.arange(seq, dtype=F32)[:, None] * inv_freq[None, :]
    cos, sin = jnp.cos(ang), jnp.sin(ang)
    pad = jnp.zeros((seq, HEAD_DIM - ROT_DIM), F32)
    c = jnp.concatenate([cos, cos, pad + 1.0], axis=1)
    s1 = jnp.concatenate([-sin, jnp.zeros_like(sin), pad], axis=1)
    s2 = jnp.concatenate([jnp.zeros_like(sin), sin, pad], axis=1)
    tab = jnp.stack([c, s1, s2])
    tab = jnp.concatenate([tab] * (LANES // HEAD_DIM), axis=2)
    return tab * (float(HEAD_DIM) ** -0.5)


def _gate_weights(w_a, w_i):
    L, nb, bw, _ = w_a.shape

    def bd(w):
        w = w.reshape(L, nb // 2, 2, bw, bw)
        z = jnp.zeros_like(w[:, :, 0])
        top = jnp.concatenate([w[:, :, 0], z], axis=-1)
        bot = jnp.concatenate([z, w[:, :, 1]], axis=-1)
        return jnp.concatenate([top, bot], axis=-2)

    return jnp.concatenate([bd(w_a), bd(w_i)], axis=-1).astype(BF16)


def kernel(x, norm_mix_pre, norm_mix_post, w_in, attn_sinks, w_attn_br, conv_w, conv_b,
           w_rg_a, b_rg_a, w_rg_i, b_rg_i, lru_lambda, w_rnn_br, w_pool_groups, pool_scale,
           w_pool_br, w_out, norm_mlp_pre, norm_mlp_post, w_mlp_up, w_mlp_down):
    B, S, D = x.shape
    depth = w_in.shape[0]
    assert S % MIX_TILE == 0 and S % MLP_TILE == 0 and MIX_TILE % ATT_BLOCK == 0
    tab = _rope_tables(S)
    rows = lambda a: a[:, None, :]
    bf = lambda a: a.astype(BF16)
    mixer_params = (
        rows(norm_mix_pre), rows(norm_mix_post), bf(w_in), attn_sinks, bf(w_attn_br), conv_w,
        rows(conv_b), _gate_weights(w_rg_a, w_rg_i), rows(b_rg_a), rows(b_rg_i),
        rows(lru_lambda), bf(w_rnn_br), bf(w_pool_groups), rows(pool_scale), bf(w_pool_br),
        bf(w_out))
    mlp_params = (rows(norm_mlp_pre), rows(norm_mlp_post), bf(w_mlp_up), bf(w_mlp_down))
    h = x
    for l in range(depth):
        h = _mixer_call(l, h, tab, *mixer_params)
        h = _mlp_call(l, h, *mlp_params)
    return h
```
